```python
import math, functools
import jax, jax.numpy as jnp
from jax import lax
import numpy as np

D_MODEL = 1024
BATCH = 8
SEQ = 2048
DEPTH = 2
DEC_BATCH = 32
DEC_SEQ = 1
PAST_LEN = 8192
PAGE_SIZE = 128

D_ATT = D_MODEL // 2
HEAD_DIM = 64
N_ATT_HEADS = D_ATT // HEAD_DIM
D_RNN = D_MODEL - D_ATT
N_RNN_BLOCKS = 8
RNN_BLOCK = D_RNN // N_RNN_BLOCKS
RNN_CONV_W = 4
RGLRU_C = 8.0
D_FF = 3 * D_MODEL
FFN_CONV_W = 3
Q_BLOCK = 128
N_IN = 3 * D_ATT + N_ATT_HEADS + 2 * D_RNN
DN_ALPHA = (2 * DEPTH) ** 0.25
DN_BETA = (8 * DEPTH) ** -0.25
EPS = 1e-5

kernel_name = 'hymba_rglru_fox_convffn_step'


def layer_norm(x, g, b):
    xf = x.astype(jnp.float32)
    mu = jnp.mean(xf, axis=-1, keepdims=True)
    var = jnp.mean(jnp.square(xf - mu), axis=-1, keepdims=True)
    y = (xf - mu) * lax.rsqrt(var + EPS) * g.astype(jnp.float32) + b.astype(jnp.float32)
    return y.astype(x.dtype)


def rms_norm(x, g):
    xf = x.astype(jnp.float32)
    y = xf * lax.rsqrt(jnp.mean(jnp.square(xf), axis=-1, keepdims=True) + EPS) * g.astype(jnp.float32)
    return y.astype(x.dtype)


def causal_dwconv(x, buf, w, b):
    width = w.shape[0]
    t = x.shape[1]
    xx = jnp.concatenate([buf.astype(x.dtype), x], axis=1)
    y = b + sum(xx[:, j:j + t] * w[j] for j in range(width))
    return y, xx[:, t:]


def rglru(x, h0, w_a, b_a, w_x, b_x, lam):
    bsz, t, _ = x.shape
    xb = x.reshape(bsz, t, N_RNN_BLOCKS, RNN_BLOCK)
    r = jax.nn.sigmoid((jnp.einsum('btnc,ncd->btnd', xb, w_a).reshape(bsz, t, D_RNN) + b_a).astype(jnp.float32))
    i = jax.nn.sigmoid((jnp.einsum('btnc,ncd->btnd', xb, w_x).reshape(bsz, t, D_RNN) + b_x).astype(jnp.float32))
    log_a = -RGLRU_C * jax.nn.softplus(-lam.astype(jnp.float32)) * r
    a = jnp.exp(log_a)
    u = jnp.sqrt(-jnp.expm1(2.0 * log_a)) * (i * x.astype(jnp.float32))

    def step(h, au):
        a_t, u_t = au
        h = a_t * h + u_t
        return h, h

    h_last, hs = lax.scan(step, h0.astype(jnp.float32), (a.swapaxes(0, 1), u.swapaxes(0, 1)))
    return hs.swapaxes(0, 1).astype(x.dtype), h_last


def forget_attention_prompt(q, k, v, logf):
    bsz, t, h, dh = q.shape
    nb = t // Q_BLOCK
    scale = dh ** -0.5
    ck = jnp.cumsum(logf, axis=1).transpose(0, 2, 1)
    qb = q.reshape(bsz, nb, Q_BLOCK, h, dh).swapaxes(0, 1)
    cqb = ck.reshape(bsz, h, nb, Q_BLOCK).transpose(2, 0, 1, 3)
    k_pos = jnp.arange(t)

    def block(args):
        qi, cqi, bi = args
        s = jnp.einsum('bqhd,bkhd->bhqk', qi, k).astype(jnp.float32) * scale + cqi[..., None] - ck[:, :, None, :]
        q_pos = bi * Q_BLOCK + jnp.arange(Q_BLOCK)
        s = jnp.where(k_pos[None, :] <= q_pos[:, None], s, -jnp.inf)
        p = jax.nn.softmax(s, axis=-1)
        return jnp.einsum('bhqk,bkhd->bqhd', p.astype(v.dtype), v)

    out = lax.map(block, (qb, cqb, jnp.arange(nb)))
    return out.swapaxes(0, 1).reshape(bsz, t, h * dh)


def forget_attention_decode(q, k, v, logf, k_past, v_past, logf_past):
    bsz, s_new, h, dh = q.shape
    n_past = k_past.shape[1]
    scale = dh ** -0.5
    c_past = jnp.cumsum(logf_past.astype(jnp.float32), axis=1)
    c_new = c_past[:, -1:] + jnp.cumsum(logf, axis=1)
    cq = c_new.transpose(0, 2, 1)[..., None]
    sc_past = jnp.einsum('bqhd,bkhd->bhqk', q, k_past).astype(jnp.float32) * scale + cq - c_past.transpose(0, 2, 1)[:, :, None, :]
    sc_new = jnp.einsum('bqhd,bkhd->bhqk', q, k).astype(jnp.float32) * scale + cq - c_new.transpose(0, 2, 1)[:, :, None, :]
    causal = jnp.arange(s_new)[None, :] <= jnp.arange(s_new)[:, None]
    sc_new = jnp.where(causal, sc_new, -jnp.inf)
    p = jax.nn.softmax(jnp.concatenate([sc_past, sc_new], axis=-1), axis=-1)
    out = (jnp.einsum('bhqk,bkhd->bqhd', p[..., :n_past].astype(v_past.dtype), v_past)
           + jnp.einsum('bhqk,bkhd->bqhd', p[..., n_past:].astype(v.dtype), v))
    return out.reshape(bsz, s_new, h * dh)


def decoder_layer(x, attend, h0, buf_r, buf_f, w_in, b_f, rnn_conv_w, rnn_conv_b, w_a, b_a, w_x, b_x, lam,
                  g_att, g_rnn, w_out, ln1_g, ln1_b, w_up, ffn_conv_w, ffn_conv_b, w_down, ln2_g, ln2_b):
    bsz, t, _ = x.shape
    proj = x @ w_in
    cuts = [D_ATT, 2 * D_ATT, 3 * D_ATT, 3 * D_ATT + N_ATT_HEADS, 3 * D_ATT + N_ATT_HEADS + D_RNN]
    q, k, v, f_logit, xr, gr = jnp.split(proj, cuts, axis=-1)
    q = q.reshape(bsz, t, N_ATT_HEADS, HEAD_DIM)
    k = k.reshape(bsz, t, N_ATT_HEADS, HEAD_DIM)
    v = v.reshape(bsz, t, N_ATT_HEADS, HEAD_DIM)
    logf = jax.nn.log_sigmoid((f_logit + b_f).astype(jnp.float32))
    y_att = attend(q, k, v, logf)
    xc, new_buf_r = causal_dwconv(xr, buf_r, rnn_conv_w, rnn_conv_b)
    y_rnn, h_last = rglru(xc, h0, w_a, b_a, w_x, b_x, lam)
    y_rnn = y_rnn * jax.nn.gelu(gr)
    mix = jnp.concatenate([rms_norm(y_att, g_att), rms_norm(y_rnn, g_rnn)], axis=-1) @ w_out
    x = layer_norm(DN_ALPHA * x + mix, ln1_g, ln1_b)
    gate, val = jnp.split(x @ w_up, 2, axis=-1)
    gate, new_buf_f = causal_dwconv(gate, buf_f, ffn_conv_w, ffn_conv_b)
    ffn = (jax.nn.gelu(gate) * val) @ w_down
    x = layer_norm(DN_ALPHA * x + ffn, ln2_g, ln2_b)
    return x, (k, v, logf, h_last, new_buf_r, new_buf_f)


def setup_inputs(seed: int = 0) -> dict:
    key = jax.random.key(seed)
    ks = jax.random.split(key, 32)
    f32 = jnp.float32

    def nrm(k, shape, s=1.0):
        return jax.random.normal(k, shape, f32) * s

    n_pages = PAST_LEN // PAGE_SIZE
    n_used = DEC_BATCH * n_pages
    n_pool = n_used + max(1, n_used // 4)
    page_table = jax.random.permutation(ks[0], n_pool)[:n_used].reshape(DEC_BATCH, n_pages).astype(jnp.int32)
    w_in = nrm(ks[9], (DEPTH, D_MODEL, N_IN), D_MODEL ** -0.5)
    w_in = w_in.at[..., 2 * D_ATT:3 * D_ATT].multiply(DN_BETA)
    u = jax.random.uniform(ks[17], (DEPTH, D_RNN), f32, 0.9, 0.999)
    p = u ** (1.0 / RGLRU_C)
    lam = jnp.log(p) - jnp.log1p(-p)
    return {
        'x_prompt': nrm(ks[1], (BATCH, SEQ, D_MODEL)),
        'x_sample': nrm(ks[2], (DEC_BATCH, DEC_SEQ, D_MODEL)),
        'cache_k': nrm(ks[3], (DEPTH, n_pool, PAGE_SIZE, N_ATT_HEADS, HEAD_DIM)),
        'cache_v': nrm(ks[4], (DEPTH, n_pool, PAGE_SIZE, N_ATT_HEADS, HEAD_DIM)),
        'cache_logf': jax.nn.log_sigmoid(2.0 + nrm(ks[5], (DEPTH, n_pool, PAGE_SIZE, N_ATT_HEADS))),
        'page_table': page_table,
        'state_h': nrm(ks[6], (DEPTH, DEC_BATCH, D_RNN), 0.5),
        'state_conv_rnn': nrm(ks[7], (DEPTH, DEC_BATCH, RNN_CONV_W - 1, D_RNN)),
        'state_conv_ffn': nrm(ks[8], (DEPTH, DEC_BATCH, FFN_CONV_W - 1, D_FF)),
        'w_in': w_in,
        'b_f': 2.0 + nrm(ks[10], (DEPTH, N_ATT_HEADS), 0.1),
        'rnn_conv_w': nrm(ks[11], (DEPTH, RNN_CONV_W, D_RNN), RNN_CONV_W ** -0.5),
        'rnn_conv_b': nrm(ks[12], (DEPTH, D_RNN), 0.01),
        'w_a': nrm(ks[13], (DEPTH, N_RNN_BLOCKS, RNN_BLOCK, RNN_BLOCK), RNN_BLOCK ** -0.5),
        'b_a': nrm(ks[14], (DEPTH, D_RNN), 0.01),
        'w_x': nrm(ks[15], (DEPTH, N_RNN_BLOCKS, RNN_BLOCK, RNN_BLOCK), RNN_BLOCK ** -0.5),
        'b_x': nrm(ks[16], (DEPTH, D_RNN), 0.01),
        'lam': lam,
        'g_att': 1.0 + nrm(ks[18], (DEPTH, D_ATT), 0.01),
        'g_rnn': 1.0 + nrm(ks[19], (DEPTH, D_RNN), 0.01),
        'w_out': nrm(ks[20], (DEPTH, D_ATT + D_RNN, D_MODEL), (D_ATT + D_RNN) ** -0.5 * DN_BETA),
        'ln1_g': 1.0 + nrm(ks[21], (DEPTH, D_MODEL), 0.01),
        'ln1_b': nrm(ks[22], (DEPTH, D_MODEL), 0.01),
        'w_up': nrm(ks[23], (DEPTH, D_MODEL, 2 * D_FF), D_MODEL ** -0.5),
        'ffn_conv_w': nrm(ks[24], (DEPTH, FFN_CONV_W, D_FF), FFN_CONV_W ** -0.5),
        'ffn_conv_b': nrm(ks[25], (DEPTH, D_FF), 0.01),
        'w_down': nrm(ks[26], (DEPTH, D_FF, D_MODEL), D_FF ** -0.5 * DN_BETA),
        'ln2_g': 1.0 + nrm(ks[27], (DEPTH, D_MODEL), 0.01),
        'ln2_b': nrm(ks[28], (DEPTH, D_MODEL), 0.01),
    }


def reference(x_prompt, x_sample, cache_k, cache_v, cache_logf, page_table, state_h, state_conv_rnn,
              state_conv_ffn, w_in, b_f, rnn_conv_w, rnn_conv_b, w_a, b_a, w_x, b_x, lam, g_att, g_rnn,
              w_out, ln1_g, ln1_b, w_up, ffn_conv_w, ffn_conv_b, w_down, ln2_g, ln2_b):
    n_pages = PAST_LEN // PAGE_SIZE
    bp = x_prompt.shape[0]
    bs = x_sample.shape[0]
    x_p, x_s = x_prompt, x_sample
    outs_p, outs_s = [], []
    for l in range(DEPTH):
        ws = (w_in[l], b_f[l], rnn_conv_w[l], rnn_conv_b[l], w_a[l], b_a[l], w_x[l], b_x[l], lam[l],
              g_att[l], g_rnn[l], w_out[l], ln1_g[l], ln1_b[l], w_up[l], ffn_conv_w[l], ffn_conv_b[l],
              w_down[l], ln2_g[l], ln2_b[l])
        h0 = jnp.zeros((bp, D_RNN), jnp.float32)
        buf_r = jnp.zeros((bp, RNN_CONV_W - 1, D_RNN), x_p.dtype)
        buf_f = jnp.zeros((bp, FFN_CONV_W - 1, D_FF), x_p.dtype)
        x_p, st_p = decoder_layer(x_p, forget_attention_prompt, h0, buf_r, buf_f, *ws)
        kp = cache_k[l][page_table].reshape(bs, n_pages * PAGE_SIZE, N_ATT_HEADS, HEAD_DIM)
        vp = cache_v[l][page_table].reshape(bs, n_pages * PAGE_SIZE, N_ATT_HEADS, HEAD_DIM)
        lp = cache_logf[l][page_table].reshape(bs, n_pages * PAGE_SIZE, N_ATT_HEADS)
        attend = functools.partial(forget_attention_decode, k_past=kp, v_past=vp, logf_past=lp)
        x_s, st_s = decoder_layer(x_s, attend, state_h[l], state_conv_rnn[l], state_conv_ffn[l], *ws)
        outs_p.append(st_p)
        outs_s.append(st_s)
    k_p = jnp.stack([o[0] for o in outs_p])
    v_p = jnp.stack([o[1] for o in outs_p])
    lf_p = jnp.stack([o[2] for o in outs_p])
    h_p = jnp.stack([o[3] for o in outs_p])
    cr_p = jnp.stack([o[4] for o in outs_p])
    cf_p = jnp.stack([o[5] for o in outs_p])
    k_s = jnp.stack([o[0] for o in outs_s])
    v_s = jnp.stack([o[1] for o in outs_s])
    lf_s = jnp.stack([o[2] for o in outs_s])
    h_s = jnp.stack([o[3] for o in outs_s])
    cr_s = jnp.stack([o[4] for o in outs_s])
    cf_s = jnp.stack([o[5] for o in outs_s])
    return (x_p, x_s, k_p, v_p, lf_p, h_p, cr_p, cf_p, k_s, v_s, lf_s, h_s, cr_s, cf_s)
```

```python
import functools

import jax
import jax.numpy as jnp
from jax import lax
from jax.experimental import pallas as pl
from jax.experimental.pallas import tpu as pltpu

F32 = jnp.float32
BF16 = jnp.bfloat16

D_MODEL = 1024
D_ATT = 512
D_RNN = 512
N_HEADS = 8
HEAD_DIM = 64
N_RNN_BLOCKS = 8
D_FF = 3072
RGLRU_C = 8.0
DEPTH = 2
PAGE_SIZE = 128
DN_ALPHA = (2 * DEPTH) ** 0.25
EPS = 1e-5
ATT_SCALE = HEAD_DIM ** -0.5
NEG_BIG = -1e30

LANES = 128
SUBLANES = 8
VMEM_LIMIT = 56 * 1024 * 1024

TM_PROJ = 512
TQ = 256
TM_FFN = 512
FC = 512
PAGES_PER_STEP = 8


def _softplus(x):
    return jnp.maximum(x, 0.0) + jnp.log1p(jnp.exp(-jnp.abs(x)))


def _log_sigmoid(x):
    return -_softplus(-x)


def _gelu(x):
    c = (2.0 / jnp.pi) ** 0.5
    return 0.5 * x * (1.0 + jnp.tanh(c * (x + 0.044715 * (x * x * x))))


def _layer_norm(x, g, b):
    mu = jnp.mean(x, axis=-1, keepdims=True)
    xc = x - mu
    var = jnp.mean(xc * xc, axis=-1, keepdims=True)
    return xc * lax.rsqrt(var + EPS) * g + b


def _rms_norm(x, g):
    return x * lax.rsqrt(jnp.mean(x * x, axis=-1, keepdims=True) + EPS) * g


def _dot(a, b):
    return jnp.dot(a, b, preferred_element_type=F32)


def _dot_nt(a, b):
    return lax.dot_general(a, b, (((1,), (1,)), ((), ())), preferred_element_type=F32)


def _prefix_sum_lanes(x, tri):
    hi = x.astype(BF16)
    r1 = x - hi.astype(F32)
    mid = r1.astype(BF16)
    lo = (r1 - mid.astype(F32)).astype(BF16)
    return _dot(hi, tri) + _dot(mid, tri) + _dot(lo, tri)


def _tri128():
    r = lax.broadcasted_iota(jnp.int32, (LANES, LANES), 0)
    c = lax.broadcasted_iota(jnp.int32, (LANES, LANES), 1)
    return (r <= c).astype(BF16)


def _rglru_gates(xc, wg_ref, bg_ref, lam_ref):
    g = _dot(xc.astype(BF16), wg_ref[...]) + bg_ref[...]
    r = jax.nn.sigmoid(g[:, :D_RNN])
    i = jax.nn.sigmoid(g[:, D_RNN:])
    log_a = (-RGLRU_C * _softplus(-lam_ref[...])) * r
    a = jnp.exp(log_a)
    mult = jnp.sqrt(-jnp.tanh(log_a) * (a * a + 1.0))
    return a, mult * (i * xc)


def _prompt_inproj_kernel(x_ref, wn_ref, wt_ref, bf_ref, cw_ref, cb_ref, wg_ref, bg_ref, lam_ref,
                          grnn_ref,
                          q_ref, kT_ref, vT_ref, kTb_ref, vTb_ref, lfT_ref, yr_ref, hl_ref, cr_ref,
                          xbuf, a_scr, u_scr, h_scr, hc_scr, *, tm, tk):
    t = pl.program_id(1)

    @pl.when(t == 0)
    def _():
        xbuf[pl.ds(0, SUBLANES), :] = jnp.zeros((SUBLANES, D_RNN), F32)
        hc_scr[...] = jnp.zeros((SUBLANES, D_RNN), F32)

    xb = x_ref[...].astype(BF16)
    yn = _dot(xb, wn_ref[...])
    yt = _dot_nt(wt_ref[...], xb)

    q_ref[...] = (yn[:, :D_ATT] * ATT_SCALE).astype(BF16)
    kT = yt[0:D_ATT]
    vT = yt[D_ATT:2 * D_ATT]
    kT_ref[...] = kT
    vT_ref[...] = vT
    for c in range(tm // tk):
        kTb_ref[c] = kT[:, c * tk:(c + 1) * tk].astype(BF16)
        vTb_ref[c] = vT[:, c * tk:(c + 1) * tk].astype(BF16)
    lfT_ref[...] = _log_sigmoid(yt[2 * D_ATT:2 * D_ATT + N_HEADS] + bf_ref[...])

    xr = yn[:, D_ATT:D_ATT + D_RNN]
    gr = yn[:, D_ATT + D_RNN:]
    xbuf[pl.ds(SUBLANES, tm), :] = xr
    xc = (cb_ref[...] + cw_ref[3:4, :] * xr
          + cw_ref[2:3, :] * xbuf[pl.ds(SUBLANES - 1, tm), :]
          + cw_ref[1:2, :] * xbuf[pl.ds(SUBLANES - 2, tm), :]
          + cw_ref[0:1, :] * xbuf[pl.ds(SUBLANES - 3, tm), :])
    tail = xbuf[pl.ds(tm, SUBLANES), :]
    xbuf[pl.ds(0, SUBLANES), :] = tail
    cr_ref[...] = tail[SUBLANES - 3:, :]

    a, u = _rglru_gates(xc, wg_ref, bg_ref, lam_ref)
    a_scr[...] = a
    u_scr[...] = u

    sub = lax.broadcasted_iota(jnp.int32, (SUBLANES, D_RNN), 0)

    def group(g, hin):
        r0 = pl.multiple_of(g * SUBLANES, SUBLANES)
        ag = a_scr[pl.ds(r0, SUBLANES), :]
        ug = u_scr[pl.ds(r0, SUBLANES), :]
        for d in (1, 2, 4):
            keep = sub >= d
            a_sh = pltpu.roll(ag, d, 0)
            u_sh = pltpu.roll(ug, d, 0)
            ug = jnp.where(keep, ag * u_sh + ug, ug)
            ag = jnp.where(keep, ag * a_sh, ag)
        hg = ag * hin + ug
        h_scr[pl.ds(r0, SUBLANES), :] = hg
        return jnp.broadcast_to(hg[SUBLANES - 1:SUBLANES, :], (SUBLANES, D_RNN))

    hin = lax.fori_loop(0, tm // SUBLANES, group, hc_scr[...], unroll=4)
    hc_scr[...] = hin
    hl_ref[...] = hin[0:1, :]

    y = h_scr[...] * _gelu(gr)
    yr_ref[...] = _rms_norm(y, grnn_ref[...]).astype(BF16)


def _prompt_inproj(x, wn, wt, bf_col, cw, cb, wg, bg, lam, grnn):
    B, T, _ = x.shape
    tm, tk = TM_PROJ, TQ
    nt = T // tm
    wt_rows = wt.shape[0]
    const = lambda b, t: (0, 0)
    kern = functools.partial(_prompt_inproj_kernel, tm=tm, tk=tk)
    return pl.pallas_call(
        kern,
        grid=(B, nt),
        in_specs=[
            pl.BlockSpec((None, tm, D_MODEL), lambda b, t: (b, t, 0)),
            pl.BlockSpec((D_MODEL, D_ATT + 2 * D_RNN), const),
            pl.BlockSpec((wt_rows, D_MODEL), const),
            pl.BlockSpec((N_HEADS, 1), const),
            pl.BlockSpec((4, D_RNN), const),
            pl.BlockSpec((1, D_RNN), const),
            pl.BlockSpec((D_RNN, 2 * D_RNN), const),
            pl.BlockSpec((1, 2 * D_RNN), const),
            pl.BlockSpec((1, D_RNN), const),
            pl.BlockSpec((1, D_RNN), const),
        ],
        out_specs=[
            pl.BlockSpec((None, tm, D_ATT), lambda b, t: (b, t, 0)),
            pl.BlockSpec((None, D_ATT, tm), lambda b, t: (b, 0, t)),
            pl.BlockSpec((None, D_ATT, tm), lambda b, t: (b, 0, t)),
            pl.BlockSpec((None, tm // tk, D_ATT, tk), lambda b, t: (b, t, 0, 0)),
            pl.BlockSpec((None, tm // tk, D_ATT, tk), lambda b, t: (b, t, 0, 0)),
            pl.BlockSpec((None, N_HEADS, tm), lambda b, t: (b, 0, t)),
            pl.BlockSpec((None, tm, D_RNN), lambda b, t: (b, t, 0)),
            pl.BlockSpec((None, 1, D_RNN), lambda b, t: (b, 0, 0)),
            pl.BlockSpec((None, 3, D_RNN), lambda b, t: (b, 0, 0)),
        ],
        out_shape=[
            jax.ShapeDtypeStruct((B, T, D_ATT), BF16),
            jax.ShapeDtypeStruct((B, D_ATT, T), F32),
            jax.ShapeDtypeStruct((B, D_ATT, T), F32),
            jax.ShapeDtypeStruct((B, T // tk, D_ATT, tk), BF16),
            jax.ShapeDtypeStruct((B, T // tk, D_ATT, tk), BF16),
            jax.ShapeDtypeStruct((B, N_HEADS, T), F32),
            jax.ShapeDtypeStruct((B, T, D_RNN), BF16),
            jax.ShapeDtypeStruct((B, 1, D_RNN), F32),
            jax.ShapeDtypeStruct((B, 3, D_RNN), F32),
        ],
        scratch_shapes=[
            pltpu.VMEM((tm + SUBLANES, D_RNN), F32),
            pltpu.VMEM((tm, D_RNN), F32),
            pltpu.VMEM((tm, D_RNN), F32),
            pltpu.VMEM((tm, D_RNN), F32),
            pltpu.VMEM((SUBLANES, D_RNN), F32),
        ],
        compiler_params=pltpu.CompilerParams(
            dimension_semantics=("arbitrary", "arbitrary"), vmem_limit_bytes=VMEM_LIMIT),
        name="prompt_inproj",
    )(x, wn, wt, bf_col, cw, cb, wg, bg, lam, grnn)


def _prompt_attn_kernel(q_ref, kT_ref, vT_ref, lfT_ref, g_ref, o_ref, cT_scr, *, tq, seq):
    qi = pl.program_id(1)

    @pl.when(qi == 0)
    def _():
        tri = _tri128()
        run = jnp.zeros((N_HEADS, 1), F32)
        for j in range(seq // LANES):
            cs = _prefix_sum_lanes(lfT_ref[:, j * LANES:(j + 1) * LANES], tri) + run
            off = (j * LANES) % tq
            cT_scr[(j * LANES) // tq, :, off:off + LANES] = cs
            run = cs[:, LANES - 1:LANES]

    lane = lax.broadcasted_iota(jnp.int32, (tq, LANES), 1)
    row = lax.broadcasted_iota(jnp.int32, (tq, tq), 0)
    col = lax.broadcasted_iota(jnp.int32, (tq, tq), 1)
    causal = col <= row
    zero_q = jnp.zeros((tq, LANES), BF16)

    outs = []
    for pair in range(N_HEADS // 2):
        rows = slice(pair * LANES, (pair + 1) * LANES)
        q2 = q_ref[:, rows]
        halves = []
        for which in range(2):
            h = 2 * pair + which
            in_head = (lane < HEAD_DIM) if which == 0 else (lane >= HEAD_DIM)
            qm = jnp.where(in_head, q2, zero_q)

            def step(kj, carry, masked, qm=qm, h=h, rows=rows):
                m, l, acc = carry
                s = _dot(qm, kT_ref[kj, rows, :]) - cT_scr[kj, h:h + 1, :]
                if masked:
                    s = jnp.where(causal, s, NEG_BIG)
                m_new = jnp.maximum(m, jnp.max(s, axis=-1, keepdims=True))
                alpha = jnp.exp(m - m_new)
                p = jnp.exp(s - m_new)
                l = alpha * l + jnp.sum(p, axis=-1, keepdims=True)
                acc = alpha * acc + _dot_nt(p.astype(BF16), vT_ref[kj, rows, :])
                return m_new, l, acc

            init = (jnp.full((tq, 1), NEG_BIG, F32), jnp.zeros((tq, 1), F32),
                    jnp.zeros((tq, LANES), F32))
            carry = lax.fori_loop(0, qi, functools.partial(step, masked=False), init)
            _, l, acc = step(qi, carry, True)
            halves.append(acc / l)
        outs.append(jnp.where(lane < HEAD_DIM, halves[0], halves[1]))
    y = jnp.concatenate(outs, axis=-1)
    o_ref[...] = _rms_norm(y, g_ref[...]).astype(BF16)


def _prompt_attn(q, kTb, vTb, lfT, g_att):
    B, T, _ = q.shape
    tq = TQ
    nk = T // tq
    kern = functools.partial(_prompt_attn_kernel, tq=tq, seq=T)
    return pl.pallas_call(
        kern,
        grid=(B, T // tq),
        in_specs=[
            pl.BlockSpec((None, tq, D_ATT), lambda b, i: (b, i, 0)),
            pl.BlockSpec((None, nk, D_ATT, tq), lambda b, i: (b, 0, 0, 0)),
            pl.BlockSpec((None, nk, D_ATT, tq), lambda b, i: (b, 0, 0, 0)),
            pl.BlockSpec((None, N_HEADS, T), lambda b, i: (b, 0, 0)),
            pl.BlockSpec((1, D_ATT), lambda b, i: (0, 0)),
        ],
        out_specs=pl.BlockSpec((None, tq, D_ATT), lambda b, i: (b, i, 0)),
        out_shape=jax.ShapeDtypeStruct((B, T, D_ATT), BF16),
        scratch_shapes=[pltpu.VMEM((nk, N_HEADS, tq), F32)],
        compiler_params=pltpu.CompilerParams(
            dimension_semantics=("arbitrary", "arbitrary"), vmem_limit_bytes=VMEM_LIMIT),
        name="prompt_attn",
    )(q, kTb, vTb, lfT, g_att)


def _mix_head(x_ref, ya_ref, yr_ref, wo_ref, l1g_ref, l1b_ref, x1_scr, x1b_scr, acc_scr):
    mix = _dot(ya_ref[...], wo_ref[0:D_ATT, :]) + _dot(yr_ref[...], wo_ref[D_ATT:, :])
    x1 = _layer_norm(DN_ALPHA * x_ref[...] + mix, l1g_ref[...], l1b_ref[...])
    x1_scr[...] = x1
    x1b_scr[...] = x1.astype(BF16)
    acc_scr[...] = jnp.zeros(acc_scr.shape, F32)


def _prompt_mix_ffn_kernel(x_ref, ya_ref, yr_ref, wo_ref, l1g_ref, l1b_ref, wug_ref, wuv_ref, cw_ref,
                           cb_ref, wd_ref, l2g_ref, l2b_ref,
                           o_ref, cf_ref,
                           x1_scr, x1b_scr, acc_scr, gbuf, carry_scr, *, tm, tiles_per_seq, nj):
    i = pl.program_id(0)
    j = pl.program_id(1)

    @pl.when(j == 0)
    def _():
        _mix_head(x_ref, ya_ref, yr_ref, wo_ref, l1g_ref, l1b_ref, x1_scr, x1b_scr, acc_scr)

    x1b = x1b_scr[...]
    gate = _dot(x1b, wug_ref[...])
    val = _dot(x1b, wuv_ref[...])

    first = (i % tiles_per_seq) == 0
    prev = jnp.where(first, 0.0, carry_scr[j])
    gbuf[pl.ds(0, SUBLANES), :] = prev
    gbuf[pl.ds(SUBLANES, tm), :] = gate
    gc = (cb_ref[...] + cw_ref[2:3, :] * gate
          + cw_ref[1:2, :] * gbuf[pl.ds(SUBLANES - 1, tm), :]
          + cw_ref[0:1, :] * gbuf[pl.ds(SUBLANES - 2, tm), :])
    tail = gbuf[pl.ds(tm, SUBLANES), :]
    carry_scr[j] = tail
    cf_ref[...] = tail[SUBLANES - 2:, :]

    hid = (_gelu(gc) * val).astype(BF16)
    acc_scr[...] += _dot(hid, wd_ref[...])

    @pl.when(j == nj - 1)
    def _():
        o_ref[...] = _layer_norm(DN_ALPHA * x1_scr[...] + acc_scr[...], l2g_ref[...], l2b_ref[...])


def _prompt_mix_ffn(x, ya, yr, wo, l1g, l1b, wu, cw, cb, wd, l2g, l2b, seq):
    M = x.shape[0]
    tm, fc = TM_FFN, FC
    nj = D_FF // fc
    tiles_per_seq = seq // tm
    B = M // seq
    kern = functools.partial(_prompt_mix_ffn_kernel, tm=tm, tiles_per_seq=tiles_per_seq, nj=nj)
    row = lambda i, j: (i, 0)
    const = lambda i, j: (0, 0)
    return pl.pallas_call(
        kern,
        grid=(M // tm, nj),
        in_specs=[
            pl.BlockSpec((tm, D_MODEL), row),
            pl.BlockSpec((tm, D_ATT), row),
            pl.BlockSpec((tm, D_RNN), row),
            pl.BlockSpec((D_MODEL, D_MODEL), const),
            pl.BlockSpec((1, D_MODEL), const),
            pl.BlockSpec((1, D_MODEL), const),
            pl.BlockSpec((D_MODEL, fc), lambda i, j: (0, j)),
            pl.BlockSpec((D_MODEL, fc), lambda i, j: (0, j + nj)),
            pl.BlockSpec((3, fc), lambda i, j: (0, j)),
            pl.BlockSpec((1, fc), lambda i, j: (0, j)),
            pl.BlockSpec((fc, D_MODEL), lambda i, j: (j, 0)),
            pl.BlockSpec((1, D_MODEL), const),
            pl.BlockSpec((1, D_MODEL), const),
        ],
        out_specs=[
            pl.BlockSpec((tm, D_MODEL), row),
            pl.BlockSpec((None, 2, fc), lambda i, j: (i, 0, j)),
        ],
        out_shape=[
            jax.ShapeDtypeStruct((M, D_MODEL), F32),
            jax.ShapeDtypeStruct((M // tm, 2, D_FF), F32),
        ],
        scratch_shapes=[
            pltpu.VMEM((tm, D_MODEL), F32),
            pltpu.VMEM((tm, D_MODEL), BF16),
            pltpu.VMEM((tm, D_MODEL), F32),
            pltpu.VMEM((tm + SUBLANES, fc), F32),
            pltpu.VMEM((nj, SUBLANES, fc), F32),
        ],
        compiler_params=pltpu.CompilerParams(
            dimension_semantics=("arbitrary", "arbitrary"), vmem_limit_bytes=VMEM_LIMIT),
        name="prompt_mix_ffn",
    )(x, ya, yr, wo, l1g, l1b, wu, wu, cw, cb, wd, l2g, l2b)


def _sample_mix_ffn_kernel(x_ref, ya_ref, yr_ref, wo_ref, l1g_ref, l1b_ref, wug_ref, wuv_ref, p0_ref,
                           p1_ref, cw_ref, cb_ref, wd_ref, l2g_ref, l2b_ref,
                           o_ref, gate_ref,
                           x1_scr, x1b_scr, acc_scr, *, nj):
    j = pl.program_id(0)

    @pl.when(j == 0)
    def _():
        _mix_head(x_ref, ya_ref, yr_ref, wo_ref, l1g_ref, l1b_ref, x1_scr, x1b_scr, acc_scr)

    x1b = x1b_scr[...]
    gate = _dot(x1b, wug_ref[...])
    val = _dot(x1b, wuv_ref[...])
    gate_ref[...] = gate
    gc = (cb_ref[...] + cw_ref[2:3, :] * gate + cw_ref[1:2, :] * p1_ref[...]
          + cw_ref[0:1, :] * p0_ref[...])
    hid = (_gelu(gc) * val).astype(BF16)
    acc_scr[...] += _dot(hid, wd_ref[...])

    @pl.when(j == nj - 1)
    def _():
        o_ref[...] = _layer_norm(DN_ALPHA * x1_scr[...] + acc_scr[...], l2g_ref[...], l2b_ref[...])


def _sample_mix_ffn(x, ya, yr, wo, l1g, l1b, wu, p0, p1, cw, cb, wd, l2g, l2b):
    n = x.shape[0]
    fc = FC
    nj = D_FF // fc
    kern = functools.partial(_sample_mix_ffn_kernel, nj=nj)
    const = lambda j: (0, 0)
    chunk = lambda j: (0, j)
    return pl.pallas_call(
        kern,
        grid=(nj,),
        in_specs=[
            pl.BlockSpec((n, D_MODEL), const),
            pl.BlockSpec((n, D_ATT), const),
            pl.BlockSpec((n, D_RNN), const),
            pl.BlockSpec((D_MODEL, D_MODEL), const),
            pl.BlockSpec((1, D_MODEL), const),
            pl.BlockSpec((1, D_MODEL), const),
            pl.BlockSpec((D_MODEL, fc), chunk),
            pl.BlockSpec((D_MODEL, fc), lambda j: (0, j + nj)),
            pl.BlockSpec((n, fc), chunk),
            pl.BlockSpec((n, fc), chunk),
            pl.BlockSpec((3, fc), chunk),
            pl.BlockSpec((1, fc), chunk),
            pl.BlockSpec((fc, D_MODEL), lambda j: (j, 0)),
            pl.BlockSpec((1, D_MODEL), const),
            pl.BlockSpec((1, D_MODEL), const),
        ],
        out_specs=[
            pl.BlockSpec((n, D_MODEL), const),
            pl.BlockSpec((n, fc), chunk),
        ],
        out_shape=[
            jax.ShapeDtypeStruct((n, D_MODEL), F32),
            jax.ShapeDtypeStruct((n, D_FF), F32),
        ],
        scratch_shapes=[
            pltpu.VMEM((n, D_MODEL), F32),
            pltpu.VMEM((n, D_MODEL), BF16),
            pltpu.VMEM((n, D_MODEL), F32),
        ],
        compiler_params=pltpu.CompilerParams(
            dimension_semantics=("arbitrary",), vmem_limit_bytes=VMEM_LIMIT),
        name="sample_mix_ffn",
    )(x, ya, yr, wo, l1g, l1b, wu, wu, p0, p1, cw, cb, wd, l2g, l2b)


def _sample_inproj_kernel(x_ref, wn_ref, wt_ref, bf_ref, cs_ref, h0_ref, cw_ref, cb_ref, wg_ref, bg_ref,
                          lam_ref, grnn_ref,
                          q_ref, k_ref, v_ref, lf_ref, yr_ref, h_ref, csn_ref):
    xb = x_ref[...].astype(BF16)
    yn = _dot(xb, wn_ref[...])
    yt = _dot_nt(xb, wt_ref[...])
    q_ref[...] = yn[:, :D_ATT]
    k_ref[...] = yt[:, 0:D_ATT]
    v_ref[...] = yt[:, D_ATT:2 * D_ATT]
    lf_ref[...] = _log_sigmoid(yt[:, 2 * D_ATT:2 * D_ATT + N_HEADS] + bf_ref[...])

    xr = yn[:, D_ATT:D_ATT + D_RNN]
    gr = yn[:, D_ATT + D_RNN:]
    xc = (cb_ref[...] + cw_ref[3:4, :] * xr + cw_ref[2:3, :] * cs_ref[2]
          + cw_ref[1:2, :] * cs_ref[1] + cw_ref[0:1, :] * cs_ref[0])
    csn_ref[0] = cs_ref[1]
    csn_ref[1] = cs_ref[2]
    csn_ref[2] = xr

    a, u = _rglru_gates(xc, wg_ref, bg_ref, lam_ref)
    h = a * h0_ref[...] + u
    h_ref[...] = h
    yr_ref[...] = _rms_norm(h * _gelu(gr), grnn_ref[...]).astype(BF16)


def _sample_inproj(x, wn, wt, bf_row, cs, h0, cw, cb, wg, bg, lam, grnn):
    n = x.shape[0]
    return pl.pallas_call(
        _sample_inproj_kernel,
        out_shape=[
            jax.ShapeDtypeStruct((n, D_ATT), F32),
            jax.ShapeDtypeStruct((n, D_ATT), F32),
            jax.ShapeDtypeStruct((n, D_ATT), F32),
            jax.ShapeDtypeStruct((n, N_HEADS), F32),
            jax.ShapeDtypeStruct((n, D_RNN), BF16),
            jax.ShapeDtypeStruct((n, D_RNN), F32),
            jax.ShapeDtypeStruct((3, n, D_RNN), F32),
        ],
        compiler_params=pltpu.CompilerParams(vmem_limit_bytes=VMEM_LIMIT),
        name="sample_inproj",
    )(x, wn, wt, bf_row, cs, h0, cw, cb, wg, bg, lam, grnn)


def _sample_attn_kernel(pt_ref, q_ref, kn_ref, vn_ref, lfn_ref, g_ref, *rest, n_pages, n_steps):
    k_refs = rest[:n_pages]
    v_refs = rest[n_pages:2 * n_pages]
    lf_refs = rest[2 * n_pages:3 * n_pages]
    o_ref = rest[3 * n_pages]
    m_scr, l_scr, acc_scr, c_scr = rest[3 * n_pages + 1:]
    j = pl.program_id(1)

    @pl.when(j == 0)
    def _():
        m_scr[...] = jnp.full(m_scr.shape, NEG_BIG, F32)
        l_scr[...] = jnp.zeros(l_scr.shape, F32)
        acc_scr[...] = jnp.zeros(acc_scr.shape, F32)
        c_scr[...] = jnp.zeros(c_scr.shape, F32)

    head_of_lane = lax.broadcasted_iota(jnp.int32, (N_HEADS, D_ATT), 1) // HEAD_DIM
    head_of_row = lax.broadcasted_iota(jnp.int32, (N_HEADS, D_ATT), 0)
    own = head_of_lane == head_of_row
    q_blk = jnp.where(own, jnp.broadcast_to(q_ref[...], (N_HEADS, D_ATT)) * ATT_SCALE, 0.0)
    q_bf = q_blk.astype(BF16)

    tri = _tri128()
    off = c_scr[...]
    scores = []
    for i in range(n_pages):
        kp = k_refs[i][...].reshape(D_ATT, PAGE_SIZE).astype(BF16)
        c = _prefix_sum_lanes(lf_refs[i][...], tri) + off
        off = c[:, PAGE_SIZE - 1:PAGE_SIZE]
        scores.append(_dot(q_bf, kp) - c)
    c_scr[...] = off
    s = jnp.concatenate(scores, axis=-1)

    m = m_scr[...]
    m_new = jnp.maximum(m, jnp.max(s, axis=-1, keepdims=True))
    alpha = jnp.exp(m - m_new)
    p = jnp.exp(s - m_new)
    l_scr[...] = alpha * l_scr[...] + jnp.sum(p, axis=-1, keepdims=True)
    m_scr[...] = m_new
    pv = jnp.zeros((N_HEADS, D_ATT), F32)
    for i in range(n_pages):
        vp = v_refs[i][...].reshape(D_ATT, PAGE_SIZE).astype(BF16)
        pv = pv + _dot_nt(p[:, i * PAGE_SIZE:(i + 1) * PAGE_SIZE].astype(BF16), vp)
    acc_scr[...] = alpha * acc_scr[...] + pv

    @pl.when(j == n_steps - 1)
    def _():
        c_new = c_scr[...] + lfn_ref[...]
        s_new = jnp.sum(jnp.where(own, q_blk * kn_ref[...], 0.0), axis=-1, keepdims=True) - c_new
        m_old = m_scr[...]
        m_fin = jnp.maximum(m_old, s_new)
        a2 = jnp.exp(m_old - m_fin)
        p_new = jnp.exp(s_new - m_fin)
        l_fin = a2 * l_scr[...] + p_new
        acc = a2 * acc_scr[...] + p_new * vn_ref[...]
        y = jnp.sum(jnp.where(own, acc / l_fin, 0.0), axis=0, keepdims=True)
        o_ref[...] = _rms_norm(y, g_ref[...]).astype(BF16)


def _sample_attn(layer, page_table, q, k_new, v_new, lf_new, g_att, cache_kT, cache_vT, cache_lfT):
    n, n_pt = page_table.shape
    P = PAGES_PER_STEP
    n_steps = n_pt // P
    kern = functools.partial(_sample_attn_kernel, n_pages=P, n_steps=n_steps)
    row3 = lambda b, j, pt: (b, 0, 0)

    def page_spec(i, shape):
        nd = len(shape)
        return pl.BlockSpec((None, None) + shape,
                            lambda b, j, pt, i=i: (layer, pt[b, j * P + i]) + (0,) * nd)

    kv_shape = (N_HEADS, HEAD_DIM, PAGE_SIZE)
    in_specs = [
        pl.BlockSpec((None, 1, D_ATT), row3),
        pl.BlockSpec((None, 1, D_ATT), row3),
        pl.BlockSpec((None, 1, D_ATT), row3),
        pl.BlockSpec((None, N_HEADS, 1), row3),
        pl.BlockSpec((1, D_ATT), lambda b, j, pt: (0, 0)),
    ]
    in_specs += [page_spec(i, kv_shape) for i in range(P)]
    in_specs += [page_spec(i, kv_shape) for i in range(P)]
    in_specs += [page_spec(i, (N_HEADS, PAGE_SIZE)) for i in range(P)]
    grid_spec = pltpu.PrefetchScalarGridSpec(
        num_scalar_prefetch=1,
        grid=(n, n_steps),
        in_specs=in_specs,
        out_specs=pl.BlockSpec((None, 1, D_ATT), row3),
        scratch_shapes=[
            pltpu.VMEM((N_HEADS, 1), F32),
            pltpu.VMEM((N_HEADS, 1), F32),
            pltpu.VMEM((N_HEADS, D_ATT), F32),
            pltpu.VMEM((N_HEADS, 1), F32),
        ],
    )
    return pl.pallas_call(
        kern,
        grid_spec=grid_spec,
        out_shape=jax.ShapeDtypeStruct((n, 1, D_ATT), BF16),
        compiler_params=pltpu.CompilerParams(
            dimension_semantics=("arbitrary", "arbitrary"), vmem_limit_bytes=VMEM_LIMIT),
        name="sample_attn",
    )(page_table, q, k_new, v_new, lf_new, g_att,
      *([cache_kT] * P), *([cache_vT] * P), *([cache_lfT] * P))


def _block_diag(w):
    n, c, d = w.shape
    return jnp.einsum("ncd,nm->ncmd", w, jnp.eye(n, dtype=w.dtype)).reshape(n * c, n * d)


def kernel(x_prompt, x_sample, cache_k, cache_v, cache_logf, page_table, state_h, state_conv_rnn,
           state_conv_ffn, w_in, b_f, rnn_conv_w, rnn_conv_b, w_a, b_a, w_x, b_x, lam, g_att, g_rnn,
           w_out, ln1_g, ln1_b, w_up, ffn_conv_w, ffn_conv_b, w_down, ln2_g, ln2_b):
    B, T, _ = x_prompt.shape
    n_s = x_sample.shape[0]

    cache_kT = jnp.transpose(cache_k, (0, 1, 3, 4, 2))
    cache_vT = jnp.transpose(cache_v, (0, 1, 3, 4, 2))
    cache_lfT = jnp.transpose(cache_logf, (0, 1, 3, 2))

    x_p = x_prompt
    x_s = x_sample.reshape(n_s, D_MODEL)
    outs = [[] for _ in range(12)]
    c1, c2, c3 = D_ATT, 2 * D_ATT, 3 * D_ATT
    c4 = c3 + N_HEADS
    c5 = c4 + D_RNN
    for l in range(DEPTH):
        wi = w_in[l]
        wn = jnp.concatenate([wi[:, :c1], wi[:, c4:c5], wi[:, c5:]], axis=1).astype(BF16)
        wt = jnp.concatenate([wi[:, c1:c4], jnp.zeros((D_MODEL, 8), F32)], axis=1).T.astype(BF16)
        wg = jnp.concatenate([_block_diag(w_a[l]), _block_diag(w_x[l])], axis=1).astype(BF16)
        bg = jnp.concatenate([b_a[l], b_x[l]])[None, :]
        wo = w_out[l].astype(BF16)
        wu = w_up[l].astype(BF16)
        wd = w_down[l].astype(BF16)
        row = lambda v: v[None, :]
        cw, cb = rnn_conv_w[l], row(rnn_conv_b[l])
        lam_l, grnn, gatt = row(lam[l]), row(g_rnn[l]), row(g_att[l])
        l1g, l1b, l2g, l2b = row(ln1_g[l]), row(ln1_b[l]), row(ln2_g[l]), row(ln2_b[l])
        fcw, fcb = ffn_conv_w[l], row(ffn_conv_b[l])

        q, kT, vT, kTb, vTb, lfT, yr, h_last, cr = _prompt_inproj(
            x_p, wn, wt, b_f[l][:, None], cw, cb, wg, bg, lam_l, grnn)
        ya = _prompt_attn(q, kTb, vTb, lfT, gatt)
        x2, cf = _prompt_mix_ffn(x_p.reshape(B * T, D_MODEL), ya.reshape(B * T, D_ATT),
                                 yr.reshape(B * T, D_RNN), wo, l1g, l1b, wu, fcw, fcb, wd, l2g, l2b, T)
        x_p = x2.reshape(B, T, D_MODEL)
        outs[0].append(kT.reshape(B, N_HEADS, HEAD_DIM, T).transpose(0, 3, 1, 2))
        outs[1].append(vT.reshape(B, N_HEADS, HEAD_DIM, T).transpose(0, 3, 1, 2))
        outs[2].append(lfT.transpose(0, 2, 1))
        outs[3].append(h_last.reshape(B, D_RNN))
        outs[4].append(cr)
        tiles_per_seq = T // TM_FFN
        outs[5].append(cf[tiles_per_seq - 1::tiles_per_seq])

        cs = jnp.transpose(state_conv_rnn[l], (1, 0, 2))
        qs, ks, vs, lfs, yrs, hs, csn = _sample_inproj(
            x_s, wn, wt, b_f[l][None, :], cs, state_h[l], cw, cb, wg, bg, lam_l, grnn)
        yas = _sample_attn(l, page_table, qs.reshape(n_s, 1, D_ATT), ks.reshape(n_s, 1, D_ATT),
                           vs.reshape(n_s, 1, D_ATT), lfs.reshape(n_s, N_HEADS, 1), gatt,
                           cache_kT, cache_vT, cache_lfT)
        p0 = state_conv_ffn[l][:, 0, :]
        p1 = state_conv_ffn[l][:, 1, :]
        x_s, gate_s = _sample_mix_ffn(x_s, yas.reshape(n_s, D_ATT), yrs, wo, l1g, l1b, wu, p0, p1,
                                      fcw, fcb, wd, l2g, l2b)
        outs[6].append(ks.reshape(n_s, 1, N_HEADS, HEAD_DIM))
        outs[7].append(vs.reshape(n_s, 1, N_HEADS, HEAD_DIM))
        outs[8].append(lfs.reshape(n_s, 1, N_HEADS))
        outs[9].append(hs)
        outs[10].append(jnp.transpose(csn, (1, 0, 2)))
        outs[11].append(jnp.stack([p1, gate_s], axis=1))

    st = [jnp.stack(o) for o in outs]
    return (x_p, x_s.reshape(n_s, 1, D_MODEL), st[0], st[1], st[2], st[3], st[4], st[5],
            st[6], st[7], st[8], st[9], st[10], st[11])
```

```python
import functools

import jax
import jax.numpy as jnp
from jax import lax
from jax.experimental import pallas as pl
from jax.experimental.pallas import tpu as pltpu

F32 = jnp.float32
BF16 = jnp.bfloat16

D_MODEL = 1024
D_ATT = 512
D_RNN = 512
N_HEADS = 8
HEAD_DIM = 64
N_RNN_BLOCKS = 8
D_FF = 3072
RGLRU_C = 8.0
DEPTH = 2
PAGE_SIZE = 128
DN_ALPHA = (2 * DEPTH) ** 0.25
EPS = 1e-5
ATT_SCALE = HEAD_DIM ** -0.5
NEG_BIG = -1e30

LANES = 128
SUBLANES = 8
VMEM_LIMIT = 56 * 1024 * 1024

TM_PROJ = 512
TQ = 256
TK = 128
TM_FFN = 512
FC = 512
PAGES_PER_STEP = 8


def _softplus(x):
    return jnp.maximum(x, 0.0) + jnp.log1p(jnp.exp(-jnp.abs(x)))


def _log_sigmoid(x):
    return -_softplus(-x)


def _gelu(x):
    c = (2.0 / jnp.pi) ** 0.5
    return 0.5 * x * (1.0 + jnp.tanh(c * (x + 0.044715 * (x * x * x))))


def _layer_norm(x, g, b):
    mu = jnp.mean(x, axis=-1, keepdims=True)
    xc = x - mu
    var = jnp.mean(xc * xc, axis=-1, keepdims=True)
    return xc * lax.rsqrt(var + EPS) * g + b


def _rms_norm(x, g):
    return x * lax.rsqrt(jnp.mean(x * x, axis=-1, keepdims=True) + EPS) * g


def _dot(a, b):
    return jnp.dot(a, b, preferred_element_type=F32)


def _dot_nt(a, b):
    return lax.dot_general(a, b, (((1,), (1,)), ((), ())), preferred_element_type=F32)


def _prefix_sum_lanes(x, tri):
    hi = x.astype(BF16)
    r1 = x - hi.astype(F32)
    mid = r1.astype(BF16)
    lo = (r1 - mid.astype(F32)).astype(BF16)
    return _dot(hi, tri) + _dot(mid, tri) + _dot(lo, tri)


def _tri128():
    r = lax.broadcasted_iota(jnp.int32, (LANES, LANES), 0)
    c = lax.broadcasted_iota(jnp.int32, (LANES, LANES), 1)
    return (r <= c).astype(BF16)


def _running_sum_blocks(blocks, run):
    tri = _tri128()
    pre = [_prefix_sum_lanes(b, tri) for b in blocks]
    out = []
    for p in pre:
        out.append(p + run)
        run = run + p[:, LANES - 1:LANES]
    return out, run


def _split3(x):
    hi = x.astype(BF16).astype(F32)
    r1 = x - hi
    mid = r1.astype(BF16).astype(F32)
    lo = (r1 - mid).astype(BF16).astype(F32)
    return hi, mid, lo


def _rglru_gates(xc, wg_ref, bg_ref, lam_ref):
    g = _dot(xc.astype(BF16), wg_ref[...]) + bg_ref[...]
    r = jax.nn.sigmoid(g[:, :D_RNN])
    i = jax.nn.sigmoid(g[:, D_RNN:])
    log_a = (-RGLRU_C * _softplus(-lam_ref[...])) * r
    a = jnp.exp(log_a)
    mult = jnp.sqrt(-jnp.tanh(log_a) * (a * a + 1.0))
    return a, mult * (i * xc)


def _prompt_inproj_kernel(*refs, tm, tq, tk, n_alias):
    (x_ref, wn_ref, wt_ref, bfc_ref, bfr_ref, cw_ref, cb_ref, wg_ref, bg_ref, lam_ref,
     grnn_ref) = refs[:11]
    (qblk_ref, kT_ref, vT_ref, kblk_ref, vblk_ref, lfT_ref, yr_ref, hl_ref, cr_ref,
     xbuf, a_scr, u_scr, h_scr, hc_scr, crun_scr) = refs[11 + n_alias:]
    t = pl.program_id(1)

    @pl.when(t == 0)
    def _():
        xbuf[pl.ds(0, SUBLANES), :] = jnp.zeros((SUBLANES, D_RNN), F32)
        hc_scr[...] = jnp.zeros((SUBLANES, D_RNN), F32)
        crun_scr[...] = jnp.zeros(crun_scr.shape, F32)

    xb = x_ref[...].astype(BF16)
    yn = _dot(xb, wn_ref[...])
    yt = _dot_nt(wt_ref[...], xb)

    qT = yt[0:D_ATT] * ATT_SCALE
    kT = yt[D_ATT:2 * D_ATT]
    vT = yt[2 * D_ATT:3 * D_ATT]
    kT_ref[...] = kT
    vT_ref[...] = vT
    lfT_ref[...] = _log_sigmoid(yt[3 * D_ATT:3 * D_ATT + N_HEADS] + bfc_ref[...])

    lf_rows = _log_sigmoid(yn[:, D_ATT:D_ATT + LANES] + bfr_ref[...])
    ri = lax.broadcasted_iota(jnp.int32, (LANES, LANES), 0)
    ci = lax.broadcasted_iota(jnp.int32, (LANES, LANES), 1)
    lower = (ci <= ri).astype(BF16)
    run = crun_scr[0:1, :]
    c_rows = []
    for j in range(tm // LANES):
        hi, mid, lo = _split3(lf_rows[j * LANES:(j + 1) * LANES])
        pre = (_dot(lower, hi.astype(BF16)) + _dot(lower, mid.astype(BF16))
               + _dot(lower, lo.astype(BF16)))
        c_rows.append(pre + run)
        run = run + pre[LANES - 1:LANES, :]
    crun_scr[...] = jnp.broadcast_to(run, crun_scr.shape)
    nhi, nmid, nlo = _split3(-jnp.concatenate(c_rows, axis=0))

    lane = lax.broadcasted_iota(jnp.int32, (tm, LANES), 1)
    is_head = lane < N_HEADS
    packed = jnp.where(is_head, nhi, 0.0)
    packed = packed + pltpu.roll(jnp.where(is_head, nmid, 0.0), N_HEADS, 1)
    packed = packed + pltpu.roll(jnp.where(is_head, nlo, 0.0), 2 * N_HEADS, 1)
    src = lax.broadcasted_iota(jnp.int32, (LANES, N_HEADS * LANES), 0)
    dst = lax.broadcasted_iota(jnp.int32, (LANES, N_HEADS * LANES), 1)
    src_head = src % N_HEADS
    src_term = src // N_HEADS
    base = jnp.where(src_head % 2 == 0, HEAD_DIM, 0)
    place = ((dst // LANES == src_head) & (dst % LANES == base + src_term)
             & (src_term < 3)).astype(BF16)
    bias_lanes = _dot(packed.astype(BF16), place)

    ones = jnp.ones((HEAD_DIM, tk), BF16)
    rowi = lax.broadcasted_iota(jnp.int32, (HEAD_DIM, tq), 0)
    sel_rows = jnp.where(rowi < 3, 1.0, 0.0).astype(BF16)
    for h in range(N_HEADS):
        pair = yn[:, (h // 2) * LANES:(h // 2 + 1) * LANES]
        own = (lane < HEAD_DIM) if h % 2 == 0 else (lane >= HEAD_DIM)
        kb = (jnp.where(own, pair, 0.0) + bias_lanes[:, h * LANES:(h + 1) * LANES]).astype(BF16)
        chan = slice(0, HEAD_DIM) if h % 2 == 0 else slice(HEAD_DIM, 2 * HEAD_DIM)
        rest = slice(HEAD_DIM, 2 * HEAD_DIM) if h % 2 == 0 else slice(0, HEAD_DIM)
        hrows = slice(h * HEAD_DIM, (h + 1) * HEAD_DIM)
        for c in range(tm // tk):
            cols = slice(c * tk, (c + 1) * tk)
            kblk_ref[c, h] = kb[cols, :]
            vblk_ref[c, h, chan, :] = vT[hrows, cols].astype(BF16)
            vblk_ref[c, h, rest, :] = ones
        for c in range(tm // tq):
            cols = slice(c * tq, (c + 1) * tq)
            qblk_ref[c, h, chan, :] = qT[hrows, cols].astype(BF16)
            qblk_ref[c, h, rest, :] = sel_rows

    xr = yn[:, D_ATT + LANES:D_ATT + LANES + D_RNN]
    gr = yn[:, D_ATT + LANES + D_RNN:]
    xbuf[pl.ds(SUBLANES, tm), :] = xr
    xc = (cb_ref[...] + cw_ref[3:4, :] * xr
          + cw_ref[2:3, :] * xbuf[pl.ds(SUBLANES - 1, tm), :]
          + cw_ref[1:2, :] * xbuf[pl.ds(SUBLANES - 2, tm), :]
          + cw_ref[0:1, :] * xbuf[pl.ds(SUBLANES - 3, tm), :])
    tail = xbuf[pl.ds(tm, SUBLANES), :]
    xbuf[pl.ds(0, SUBLANES), :] = tail
    cr_ref[...] = tail[SUBLANES - 3:, :]

    a, u = _rglru_gates(xc, wg_ref, bg_ref, lam_ref)
    a_scr[...] = a
    u_scr[...] = u

    sub = lax.broadcasted_iota(jnp.int32, (SUBLANES, D_RNN), 0)

    def group(g, hin):
        r0 = pl.multiple_of(g * SUBLANES, SUBLANES)
        ag = a_scr[pl.ds(r0, SUBLANES), :]
        ug = u_scr[pl.ds(r0, SUBLANES), :]
        for d in (1, 2, 4):
            keep = sub >= d
            a_sh = pltpu.roll(ag, d, 0)
            u_sh = pltpu.roll(ug, d, 0)
            ug = jnp.where(keep, ag * u_sh + ug, ug)
            ag = jnp.where(keep, ag * a_sh, ag)
        hg = ag * hin + ug
        h_scr[pl.ds(r0, SUBLANES), :] = hg
        return jnp.broadcast_to(hg[SUBLANES - 1:SUBLANES, :], (SUBLANES, D_RNN))

    hin = lax.fori_loop(0, tm // SUBLANES, group, hc_scr[...], unroll=4)
    hc_scr[...] = hin
    hl_ref[...] = hin[0:1, :]

    y = h_scr[...] * _gelu(gr)
    yr_ref[...] = _rms_norm(y, grnn_ref[...]).astype(BF16)


def _prompt_inproj(layer, x, wn, wt, bf_col, bf_row, cw, cb, wg, bg, lam, grnn, stacked):
    B, T, _ = x.shape
    tm, tq, tk = TM_PROJ, TQ, TK
    nt = T // tm
    wt_rows = wt.shape[0]
    wn_cols = wn.shape[1]
    const = lambda b, t: (0, 0)
    n_alias = 0 if stacked is None else 3
    kern = functools.partial(_prompt_inproj_kernel, tm=tm, tq=tq, tk=tk, n_alias=n_alias)
    in_specs = [
        pl.BlockSpec((None, tm, D_MODEL), lambda b, t: (b, t, 0)),
        pl.BlockSpec((D_MODEL, wn_cols), const),
        pl.BlockSpec((wt_rows, D_MODEL), const),
        pl.BlockSpec((N_HEADS, 1), const),
        pl.BlockSpec((1, LANES), const),
        pl.BlockSpec((4, D_RNN), const),
        pl.BlockSpec((1, D_RNN), const),
        pl.BlockSpec((D_RNN, 2 * D_RNN), const),
        pl.BlockSpec((1, 2 * D_RNN), const),
        pl.BlockSpec((1, D_RNN), const),
        pl.BlockSpec((1, D_RNN), const),
    ] + [pl.BlockSpec(memory_space=pl.ANY)] * n_alias
    args = (x, wn, wt, bf_col, bf_row, cw, cb, wg, bg, lam, grnn)
    args += () if stacked is None else tuple(stacked)
    q_blk = (None, tm // tq, N_HEADS, 2 * HEAD_DIM, tq)
    v_blk = (None, tm // tk, N_HEADS, 2 * HEAD_DIM, tk)
    k_blk = (None, tm // tk, N_HEADS, tk, 2 * HEAD_DIM)
    blk_map = lambda b, t: (b, t, 0, 0, 0)
    return pl.pallas_call(
        kern,
        grid=(B, nt),
        in_specs=in_specs,
        out_specs=[
            pl.BlockSpec(q_blk, blk_map),
            pl.BlockSpec((None, None, D_ATT, tm), lambda b, t: (layer, b, 0, t)),
            pl.BlockSpec((None, None, D_ATT, tm), lambda b, t: (layer, b, 0, t)),
            pl.BlockSpec(k_blk, blk_map),
            pl.BlockSpec(v_blk, blk_map),
            pl.BlockSpec((None, None, N_HEADS, tm), lambda b, t: (layer, b, 0, t)),
            pl.BlockSpec((None, tm, D_RNN), lambda b, t: (b, t, 0)),
            pl.BlockSpec((None, 1, D_RNN), lambda b, t: (b, 0, 0)),
            pl.BlockSpec((None, 3, D_RNN), lambda b, t: (b, 0, 0)),
        ],
        out_shape=[
            jax.ShapeDtypeStruct((B, T // tq, N_HEADS, 2 * HEAD_DIM, tq), BF16),
            jax.ShapeDtypeStruct((DEPTH, B, D_ATT, T), F32),
            jax.ShapeDtypeStruct((DEPTH, B, D_ATT, T), F32),
            jax.ShapeDtypeStruct((B, T // tk, N_HEADS, tk, 2 * HEAD_DIM), BF16),
            jax.ShapeDtypeStruct((B, T // tk, N_HEADS, 2 * HEAD_DIM, tk), BF16),
            jax.ShapeDtypeStruct((DEPTH, B, N_HEADS, T), F32),
            jax.ShapeDtypeStruct((B, T, D_RNN), BF16),
            jax.ShapeDtypeStruct((B, 1, D_RNN), F32),
            jax.ShapeDtypeStruct((B, 3, D_RNN), F32),
        ],
        scratch_shapes=[
            pltpu.VMEM((tm + SUBLANES, D_RNN), F32),
            pltpu.VMEM((tm, D_RNN), F32),
            pltpu.VMEM((tm, D_RNN), F32),
            pltpu.VMEM((tm, D_RNN), F32),
            pltpu.VMEM((SUBLANES, D_RNN), F32),
            pltpu.VMEM((N_HEADS, LANES), F32),
        ],
        input_output_aliases={} if stacked is None else {11: 1, 12: 2, 13: 5},
        compiler_params=pltpu.CompilerParams(
            dimension_semantics=("arbitrary", "arbitrary"), vmem_limit_bytes=VMEM_LIMIT),
        name="prompt_inproj",
    )(*args)


def _prompt_attn_kernel(q_ref, k_ref, v_ref, g_ref, o_ref, m_scr, acc_scr, *, tq, tk):
    qi = pl.program_id(1)
    kpq = tq // tk
    key = lax.broadcasted_iota(jnp.int32, (tk, tq), 0)
    qry = lax.broadcasted_iota(jnp.int32, (tk, tq), 1)

    for h in range(N_HEADS):
        m_scr[h] = jnp.full((1, tq), NEG_BIG, F32)
        acc_scr[h] = jnp.zeros((LANES, tq), F32)

    def tile(kj, diag):
        for h in range(N_HEADS):
            s = _dot(k_ref[kj, h], q_ref[h])
            if diag is not None:
                s = jnp.where(key + diag * tk <= qry, s, NEG_BIG)
            m = m_scr[h]
            m_new = jnp.maximum(m, jnp.max(s, axis=0, keepdims=True))
            alpha = jnp.exp(m - m_new)
            p = jnp.exp(s - m_new).astype(BF16)
            acc_scr[h] = alpha * acc_scr[h] + _dot(v_ref[kj, h], p)
            m_scr[h] = m_new

    def full_tiles(j, carry):
        for d in range(kpq):
            tile(j * kpq + d, None)
        return carry

    lax.fori_loop(0, qi, full_tiles, 0)
    for d in range(kpq):
        tile(qi * kpq + d, d)

    halves = []
    for h in range(N_HEADS):
        acc = acc_scr[h]
        if h % 2 == 0:
            halves.append(acc[:HEAD_DIM] / acc[HEAD_DIM:])
        else:
            halves.append(acc[HEAD_DIM:] / acc[:HEAD_DIM])
    yT = jnp.concatenate(halves, axis=0)
    ms = jnp.mean(yT * yT, axis=0, keepdims=True)
    ynT = (yT * lax.rsqrt(ms + EPS) * g_ref[...]).astype(BF16)
    er = lax.broadcasted_iota(jnp.int32, (tq, tq), 0)
    ec = lax.broadcasted_iota(jnp.int32, (tq, tq), 1)
    o_ref[...] = _dot_nt((er == ec).astype(BF16), ynT).astype(BF16)


def _prompt_attn(qblk, kblk, vblk, g_col):
    B, nq, _, _, tq = qblk.shape
    nk, tk = kblk.shape[1], kblk.shape[3]
    T = nq * tq
    kern = functools.partial(_prompt_attn_kernel, tq=tq, tk=tk)
    return pl.pallas_call(
        kern,
        grid=(B, nq),
        in_specs=[
            pl.BlockSpec((None, None, N_HEADS, 2 * HEAD_DIM, tq), lambda b, i: (b, i, 0, 0, 0)),
            pl.BlockSpec((None, nk, N_HEADS, tk, 2 * HEAD_DIM), lambda b, i: (b, 0, 0, 0, 0)),
            pl.BlockSpec((None, nk, N_HEADS, 2 * HEAD_DIM, tk), lambda b, i: (b, 0, 0, 0, 0)),
            pl.BlockSpec((D_ATT, 1), lambda b, i: (0, 0)),
        ],
        out_specs=pl.BlockSpec((None, tq, D_ATT), lambda b, i: (b, i, 0)),
        out_shape=jax.ShapeDtypeStruct((B, T, D_ATT), BF16),
        scratch_shapes=[
            pltpu.VMEM((N_HEADS, 1, tq), F32),
            pltpu.VMEM((N_HEADS, LANES, tq), F32),
        ],
        compiler_params=pltpu.CompilerParams(
            dimension_semantics=("arbitrary", "arbitrary"), vmem_limit_bytes=VMEM_LIMIT),
        name="prompt_attn",
    )(qblk, kblk, vblk, g_col)


def _mix_head(x_ref, ya_ref, yr_ref, wo_ref, l1g_ref, l1b_ref, x1_scr, x1b_scr, acc_scr):
    mix = _dot(ya_ref[...], wo_ref[0:D_ATT, :]) + _dot(yr_ref[...], wo_ref[D_ATT:, :])
    x1 = _layer_norm(DN_ALPHA * x_ref[...] + mix, l1g_ref[...], l1b_ref[...])
    x1_scr[...] = x1
    x1b_scr[...] = x1.astype(BF16)
    acc_scr[...] = jnp.zeros(acc_scr.shape, F32)


def _prompt_mix_ffn_kernel(x_ref, ya_ref, yr_ref, wo_ref, l1g_ref, l1b_ref, wug_ref, wuv_ref, cw_ref,
                           cb_ref, wd_ref, l2g_ref, l2b_ref,
                           o_ref, cf_ref,
                           x1_scr, x1b_scr, acc_scr, gbuf, carry_scr, *, tm, tiles_per_seq, nj):
    i = pl.program_id(0)
    j = pl.program_id(1)

    @pl.when(j == 0)
    def _():
        _mix_head(x_ref, ya_ref, yr_ref, wo_ref, l1g_ref, l1b_ref, x1_scr, x1b_scr, acc_scr)

    x1b = x1b_scr[...]
    gate = _dot(x1b, wug_ref[...])
    val = _dot(x1b, wuv_ref[...])

    first = (i % tiles_per_seq) == 0
    prev = jnp.where(first, 0.0, carry_scr[j])
    gbuf[pl.ds(0, SUBLANES), :] = prev
    gbuf[pl.ds(SUBLANES, tm), :] = gate
    gc = (cb_ref[...] + cw_ref[2:3, :] * gate
          + cw_ref[1:2, :] * gbuf[pl.ds(SUBLANES - 1, tm), :]
          + cw_ref[0:1, :] * gbuf[pl.ds(SUBLANES - 2, tm), :])
    tail = gbuf[pl.ds(tm, SUBLANES), :]
    carry_scr[j] = tail
    cf_ref[...] = tail[SUBLANES - 2:, :]

    hid = (_gelu(gc) * val).astype(BF16)
    acc_scr[...] += _dot(hid, wd_ref[...])

    @pl.when(j == nj - 1)
    def _():
        o_ref[...] = _layer_norm(DN_ALPHA * x1_scr[...] + acc_scr[...], l2g_ref[...], l2b_ref[...])


def _prompt_mix_ffn(x, ya, yr, wo, l1g, l1b, wu, cw, cb, wd, l2g, l2b, seq):
    M = x.shape[0]
    tm, fc = TM_FFN, FC
    nj = D_FF // fc
    tiles_per_seq = seq // tm
    B = M // seq
    kern = functools.partial(_prompt_mix_ffn_kernel, tm=tm, tiles_per_seq=tiles_per_seq, nj=nj)
    row = lambda i, j: (i, 0)
    const = lambda i, j: (0, 0)
    return pl.pallas_call(
        kern,
        grid=(M // tm, nj),
        in_specs=[
            pl.BlockSpec((tm, D_MODEL), row),
            pl.BlockSpec((tm, D_ATT), row),
            pl.BlockSpec((tm, D_RNN), row),
            pl.BlockSpec((D_MODEL, D_MODEL), const),
            pl.BlockSpec((1, D_MODEL), const),
            pl.BlockSpec((1, D_MODEL), const),
            pl.BlockSpec((D_MODEL, fc), lambda i, j: (0, j)),
            pl.BlockSpec((D_MODEL, fc), lambda i, j: (0, j + nj)),
            pl.BlockSpec((3, fc), lambda i, j: (0, j)),
            pl.BlockSpec((1, fc), lambda i, j: (0, j)),
            pl.BlockSpec((fc, D_MODEL), lambda i, j: (j, 0)),
            pl.BlockSpec((1, D_MODEL), const),
            pl.BlockSpec((1, D_MODEL), const),
        ],
        out_specs=[
            pl.BlockSpec((tm, D_MODEL), row),
            pl.BlockSpec((None, 2, fc), lambda i, j: (i, 0, j)),
        ],
        out_shape=[
            jax.ShapeDtypeStruct((M, D_MODEL), F32),
            jax.ShapeDtypeStruct((M // tm, 2, D_FF), F32),
        ],
        scratch_shapes=[
            pltpu.VMEM((tm, D_MODEL), F32),
            pltpu.VMEM((tm, D_MODEL), BF16),
            pltpu.VMEM((tm, D_MODEL), F32),
            pltpu.VMEM((tm + SUBLANES, fc), F32),
            pltpu.VMEM((nj, SUBLANES, fc), F32),
        ],
        compiler_params=pltpu.CompilerParams(
            dimension_semantics=("arbitrary", "arbitrary"), vmem_limit_bytes=VMEM_LIMIT),
        name="prompt_mix_ffn",
    )(x, ya, yr, wo, l1g, l1b, wu, wu, cw, cb, wd, l2g, l2b)


def _sample_mix_ffn_kernel(x_ref, ya_ref, yr_ref, wo_ref, l1g_ref, l1b_ref, wug_ref, wuv_ref, p0_ref,
                           p1_ref, cw_ref, cb_ref, wd_ref, l2g_ref, l2b_ref,
                           o_ref, gate_ref,
                           x1_scr, x1b_scr, acc_scr, *, nj):
    j = pl.program_id(0)

    @pl.when(j == 0)
    def _():
        _mix_head(x_ref, ya_ref, yr_ref, wo_ref, l1g_ref, l1b_ref, x1_scr, x1b_scr, acc_scr)

    x1b = x1b_scr[...]
    gate = _dot(x1b, wug_ref[...])
    val = _dot(x1b, wuv_ref[...])
    gate_ref[...] = gate
    gc = (cb_ref[...] + cw_ref[2:3, :] * gate + cw_ref[1:2, :] * p1_ref[...]
          + cw_ref[0:1, :] * p0_ref[...])
    hid = (_gelu(gc) * val).astype(BF16)
    acc_scr[...] += _dot(hid, wd_ref[...])

    @pl.when(j == nj - 1)
    def _():
        o_ref[...] = _layer_norm(DN_ALPHA * x1_scr[...] + acc_scr[...], l2g_ref[...], l2b_ref[...])


def _sample_mix_ffn(x, ya, yr, wo, l1g, l1b, wu, p0, p1, cw, cb, wd, l2g, l2b):
    n = x.shape[0]
    fc = FC
    nj = D_FF // fc
    kern = functools.partial(_sample_mix_ffn_kernel, nj=nj)
    const = lambda j: (0, 0)
    chunk = lambda j: (0, j)
    return pl.pallas_call(
        kern,
        grid=(nj,),
        in_specs=[
            pl.BlockSpec((n, D_MODEL), const),
            pl.BlockSpec((n, D_ATT), const),
            pl.BlockSpec((n, D_RNN), const),
            pl.BlockSpec((D_MODEL, D_MODEL), const),
            pl.BlockSpec((1, D_MODEL), const),
            pl.BlockSpec((1, D_MODEL), const),
            pl.BlockSpec((D_MODEL, fc), chunk),
            pl.BlockSpec((D_MODEL, fc), lambda j: (0, j + nj)),
            pl.BlockSpec((n, fc), chunk),
            pl.BlockSpec((n, fc), chunk),
            pl.BlockSpec((3, fc), chunk),
            pl.BlockSpec((1, fc), chunk),
            pl.BlockSpec((fc, D_MODEL), lambda j: (j, 0)),
            pl.BlockSpec((1, D_MODEL), const),
            pl.BlockSpec((1, D_MODEL), const),
        ],
        out_specs=[
            pl.BlockSpec((n, D_MODEL), const),
            pl.BlockSpec((n, fc), chunk),
        ],
        out_shape=[
            jax.ShapeDtypeStruct((n, D_MODEL), F32),
            jax.ShapeDtypeStruct((n, D_FF), F32),
        ],
        scratch_shapes=[
            pltpu.VMEM((n, D_MODEL), F32),
            pltpu.VMEM((n, D_MODEL), BF16),
            pltpu.VMEM((n, D_MODEL), F32),
        ],
        compiler_params=pltpu.CompilerParams(
            dimension_semantics=("arbitrary",), vmem_limit_bytes=VMEM_LIMIT),
        name="sample_mix_ffn",
    )(x, ya, yr, wo, l1g, l1b, wu, wu, p0, p1, cw, cb, wd, l2g, l2b)


def _sample_inproj_kernel(x_ref, wn_ref, wt_ref, bf_ref, cs_ref, h0_ref, cw_ref, cb_ref, wg_ref, bg_ref,
                          lam_ref, grnn_ref,
                          q_ref, k_ref, v_ref, lf_ref, yr_ref, h_ref, csn_ref):
    xb = x_ref[...].astype(BF16)
    yn = _dot(xb, wn_ref[...])
    yt = _dot_nt(xb, wt_ref[...])
    q_ref[...] = yt[:, 0:D_ATT]
    k_ref[...] = yt[:, D_ATT:2 * D_ATT]
    v_ref[...] = yt[:, 2 * D_ATT:3 * D_ATT]
    lf_ref[...] = _log_sigmoid(yt[:, 3 * D_ATT:3 * D_ATT + N_HEADS] + bf_ref[...])

    xr = yn[:, D_ATT + LANES:D_ATT + LANES + D_RNN]
    gr = yn[:, D_ATT + LANES + D_RNN:]
    xc = (cb_ref[...] + cw_ref[3:4, :] * xr + cw_ref[2:3, :] * cs_ref[2]
          + cw_ref[1:2, :] * cs_ref[1] + cw_ref[0:1, :] * cs_ref[0])
    csn_ref[0] = cs_ref[1]
    csn_ref[1] = cs_ref[2]
    csn_ref[2] = xr

    a, u = _rglru_gates(xc, wg_ref, bg_ref, lam_ref)
    h = a * h0_ref[...] + u
    h_ref[...] = h
    yr_ref[...] = _rms_norm(h * _gelu(gr), grnn_ref[...]).astype(BF16)


def _sample_inproj(x, wn, wt, bf_row, cs, h0, cw, cb, wg, bg, lam, grnn):
    n = x.shape[0]
    return pl.pallas_call(
        _sample_inproj_kernel,
        out_shape=[
            jax.ShapeDtypeStruct((n, D_ATT), F32),
            jax.ShapeDtypeStruct((n, D_ATT), F32),
            jax.ShapeDtypeStruct((n, D_ATT), F32),
            jax.ShapeDtypeStruct((n, N_HEADS), F32),
            jax.ShapeDtypeStruct((n, D_RNN), BF16),
            jax.ShapeDtypeStruct((n, D_RNN), F32),
            jax.ShapeDtypeStruct((3, n, D_RNN), F32),
        ],
        compiler_params=pltpu.CompilerParams(vmem_limit_bytes=VMEM_LIMIT),
        name="sample_inproj",
    )(x, wn, wt, bf_row, cs, h0, cw, cb, wg, bg, lam, grnn)


def _sample_attn_kernel(pt_ref, q_ref, kn_ref, vn_ref, lfn_ref, g_ref, *rest, n_pages, n_steps):
    k_refs = rest[:n_pages]
    v_refs = rest[n_pages:2 * n_pages]
    lf_refs = rest[2 * n_pages:3 * n_pages]
    o_ref = rest[3 * n_pages]
    m_scr, l_scr, acc_scr, c_scr = rest[3 * n_pages + 1:]
    j = pl.program_id(1)

    @pl.when(j == 0)
    def _():
        m_scr[...] = jnp.full(m_scr.shape, NEG_BIG, F32)
        l_scr[...] = jnp.zeros(l_scr.shape, F32)
        acc_scr[...] = jnp.zeros(acc_scr.shape, F32)
        c_scr[...] = jnp.zeros(c_scr.shape, F32)

    head_of_lane = lax.broadcasted_iota(jnp.int32, (N_HEADS, D_ATT), 1) // HEAD_DIM
    head_of_row = lax.broadcasted_iota(jnp.int32, (N_HEADS, D_ATT), 0)
    own = head_of_lane == head_of_row
    q_blk = jnp.where(own, jnp.broadcast_to(q_ref[...], (N_HEADS, D_ATT)) * ATT_SCALE, 0.0)
    q_bf = q_blk.astype(BF16)

    cs, run = _running_sum_blocks([lf_refs[i][...] for i in range(n_pages)], c_scr[...])
    c_scr[...] = run
    scores = []
    for i in range(n_pages):
        kp = k_refs[i][...].reshape(D_ATT, PAGE_SIZE).astype(BF16)
        scores.append(_dot(q_bf, kp) - cs[i])
    s = jnp.concatenate(scores, axis=-1)

    m = m_scr[...]
    m_new = jnp.maximum(m, jnp.max(s, axis=-1, keepdims=True))
    alpha = jnp.exp(m - m_new)
    p = jnp.exp(s - m_new)
    l_scr[...] = alpha * l_scr[...] + jnp.sum(p, axis=-1, keepdims=True)
    m_scr[...] = m_new
    pv = jnp.zeros((N_HEADS, D_ATT), F32)
    for i in range(n_pages):
        vp = v_refs[i][...].reshape(D_ATT, PAGE_SIZE).astype(BF16)
        pv = pv + _dot_nt(p[:, i * PAGE_SIZE:(i + 1) * PAGE_SIZE].astype(BF16), vp)
    acc_scr[...] = alpha * acc_scr[...] + pv

    @pl.when(j == n_steps - 1)
    def _():
        c_new = c_scr[...] + lfn_ref[...]
        s_new = jnp.sum(jnp.where(own, q_blk * kn_ref[...], 0.0), axis=-1, keepdims=True) - c_new
        m_old = m_scr[...]
        m_fin = jnp.maximum(m_old, s_new)
        a2 = jnp.exp(m_old - m_fin)
        p_new = jnp.exp(s_new - m_fin)
        l_fin = a2 * l_scr[...] + p_new
        acc = a2 * acc_scr[...] + p_new * vn_ref[...]
        y = jnp.sum(jnp.where(own, acc / l_fin, 0.0), axis=0, keepdims=True)
        o_ref[...] = _rms_norm(y, g_ref[...]).astype(BF16)


def _sample_attn(layer, page_table, q, k_new, v_new, lf_new, g_att, cache_kT, cache_vT, cache_lfT):
    n, n_pt = page_table.shape
    P = PAGES_PER_STEP
    n_steps = n_pt // P
    kern = functools.partial(_sample_attn_kernel, n_pages=P, n_steps=n_steps)
    row3 = lambda b, j, pt: (b, 0, 0)

    def page_spec(i, shape):
        nd = len(shape)
        return pl.BlockSpec((None, None) + shape,
                            lambda b, j, pt, i=i: (layer, pt[b, j * P + i]) + (0,) * nd)

    kv_shape = (N_HEADS, HEAD_DIM, PAGE_SIZE)
    in_specs = [
        pl.BlockSpec((None, 1, D_ATT), row3),
        pl.BlockSpec((None, 1, D_ATT), row3),
        pl.BlockSpec((None, 1, D_ATT), row3),
        pl.BlockSpec((None, N_HEADS, 1), row3),
        pl.BlockSpec((1, D_ATT), lambda b, j, pt: (0, 0)),
    ]
    in_specs += [page_spec(i, kv_shape) for i in range(P)]
    in_specs += [page_spec(i, kv_shape) for i in range(P)]
    in_specs += [page_spec(i, (N_HEADS, PAGE_SIZE)) for i in range(P)]
    grid_spec = pltpu.PrefetchScalarGridSpec(
        num_scalar_prefetch=1,
        grid=(n, n_steps),
        in_specs=in_specs,
        out_specs=pl.BlockSpec((None, 1, D_ATT), row3),
        scratch_shapes=[
            pltpu.VMEM((N_HEADS, 1), F32),
            pltpu.VMEM((N_HEADS, 1), F32),
            pltpu.VMEM((N_HEADS, D_ATT), F32),
            pltpu.VMEM((N_HEADS, 1), F32),
        ],
    )
    return pl.pallas_call(
        kern,
        grid_spec=grid_spec,
        out_shape=jax.ShapeDtypeStruct((n, 1, D_ATT), BF16),
        compiler_params=pltpu.CompilerParams(
            dimension_semantics=("arbitrary", "arbitrary"), vmem_limit_bytes=VMEM_LIMIT),
        name="sample_attn",
    )(page_table, q, k_new, v_new, lf_new, g_att,
      *([cache_kT] * P), *([cache_vT] * P), *([cache_lfT] * P))


def _block_diag(w):
    n, c, d = w.shape
    return jnp.einsum("ncd,nm->ncmd", w, jnp.eye(n, dtype=w.dtype)).reshape(n * c, n * d)


def kernel(x_prompt, x_sample, cache_k, cache_v, cache_logf, page_table, state_h, state_conv_rnn,
           state_conv_ffn, w_in, b_f, rnn_conv_w, rnn_conv_b, w_a, b_a, w_x, b_x, lam, g_att, g_rnn,
           w_out, ln1_g, ln1_b, w_up, ffn_conv_w, ffn_conv_b, w_down, ln2_g, ln2_b):
    B, T, _ = x_prompt.shape
    n_s = x_sample.shape[0]

    cache_kT = jnp.transpose(cache_k, (0, 1, 3, 4, 2))
    cache_vT = jnp.transpose(cache_v, (0, 1, 3, 4, 2))
    cache_lfT = jnp.transpose(cache_logf, (0, 1, 3, 2))

    x_p = x_prompt
    x_s = x_sample.reshape(n_s, D_MODEL)
    outs = [[] for _ in range(12)]
    stacked = None
    c1, c2, c3 = D_ATT, 2 * D_ATT, 3 * D_ATT
    c4 = c3 + N_HEADS
    c5 = c4 + D_RNN
    for l in range(DEPTH):
        wi = w_in[l]
        wn = jnp.concatenate([wi[:, c1:c2], wi[:, c3:c4], jnp.zeros((D_MODEL, LANES - N_HEADS), F32),
                              wi[:, c4:c5], wi[:, c5:]], axis=1).astype(BF16)
        wt = jnp.concatenate([wi[:, :c4], jnp.zeros((D_MODEL, 8), F32)], axis=1).T.astype(BF16)
        bf_row = jnp.concatenate([b_f[l], jnp.zeros((LANES - N_HEADS,), F32)])[None, :]
        wg = jnp.concatenate([_block_diag(w_a[l]), _block_diag(w_x[l])], axis=1).astype(BF16)
        bg = jnp.concatenate([b_a[l], b_x[l]])[None, :]
        wo = w_out[l].astype(BF16)
        wu = w_up[l].astype(BF16)
        wd = w_down[l].astype(BF16)
        row = lambda v: v[None, :]
        cw, cb = rnn_conv_w[l], row(rnn_conv_b[l])
        lam_l, grnn, gatt = row(lam[l]), row(g_rnn[l]), row(g_att[l])
        l1g, l1b, l2g, l2b = row(ln1_g[l]), row(ln1_b[l]), row(ln2_g[l]), row(ln2_b[l])
        fcw, fcb = ffn_conv_w[l], row(ffn_conv_b[l])

        qblk, kT_all, vT_all, kblk, vblk, lfT_all, yr, h_last, cr = _prompt_inproj(
            l, x_p, wn, wt, b_f[l][:, None], bf_row, cw, cb, wg, bg, lam_l, grnn, stacked)
        stacked = (kT_all, vT_all, lfT_all)
        ya = _prompt_attn(qblk, kblk, vblk, g_att[l][:, None])
        x2, cf = _prompt_mix_ffn(x_p.reshape(B * T, D_MODEL), ya.reshape(B * T, D_ATT),
                                 yr.reshape(B * T, D_RNN), wo, l1g, l1b, wu, fcw, fcb, wd, l2g, l2b, T)
        x_p = x2.reshape(B, T, D_MODEL)
        outs[3].append(h_last.reshape(B, D_RNN))
        outs[4].append(cr)
        tiles_per_seq = T // TM_FFN
        outs[5].append(cf[tiles_per_seq - 1::tiles_per_seq])

        cs = jnp.transpose(state_conv_rnn[l], (1, 0, 2))
        qs, ks, vs, lfs, yrs, hs, csn = _sample_inproj(
            x_s, wn, wt, b_f[l][None, :], cs, state_h[l], cw, cb, wg, bg, lam_l, grnn)
        yas = _sample_attn(l, page_table, qs.reshape(n_s, 1, D_ATT), ks.reshape(n_s, 1, D_ATT),
                           vs.reshape(n_s, 1, D_ATT), lfs.reshape(n_s, N_HEADS, 1), gatt,
                           cache_kT, cache_vT, cache_lfT)
        p0 = state_conv_ffn[l][:, 0, :]
        p1 = state_conv_ffn[l][:, 1, :]
        x_s, gate_s = _sample_mix_ffn(x_s, yas.reshape(n_s, D_ATT), yrs, wo, l1g, l1b, wu, p0, p1,
                                      fcw, fcb, wd, l2g, l2b)
        outs[6].append(ks.reshape(n_s, 1, N_HEADS, HEAD_DIM))
        outs[7].append(vs.reshape(n_s, 1, N_HEADS, HEAD_DIM))
        outs[8].append(lfs.reshape(n_s, 1, N_HEADS))
        outs[9].append(hs)
        outs[10].append(jnp.transpose(csn, (1, 0, 2)))
        outs[11].append(jnp.stack([p1, gate_s], axis=1))

    st = [jnp.stack(o) for o in outs[3:]]
    kT_all, vT_all, lfT_all = stacked
    k_p = kT_all.reshape(DEPTH, B, N_HEADS, HEAD_DIM, T).transpose(0, 1, 4, 2, 3)
    v_p = vT_all.reshape(DEPTH, B, N_HEADS, HEAD_DIM, T).transpose(0, 1, 4, 2, 3)
    lf_p = lfT_all.transpose(0, 1, 3, 2)
    return (x_p, x_s.reshape(n_s, 1, D_MODEL), k_p, v_p, lf_p, *st)
```

```python
import functools

import jax
import jax.numpy as jnp
from jax import lax
from jax.experimental import pallas as pl
from jax.experimental.pallas import tpu as pltpu

F32 = jnp.float32
BF16 = jnp.bfloat16

D_MODEL = 1024
D_ATT = 512
D_RNN = 512
N_HEADS = 8
HEAD_DIM = 64
N_RNN_BLOCKS = 8
D_FF = 3072
RGLRU_C = 8.0
DEPTH = 2
PAGE_SIZE = 128
DN_ALPHA = (2 * DEPTH) ** 0.25
EPS = 1e-5
ATT_SCALE = HEAD_DIM ** -0.5
NEG_BIG = -1e30

LANES = 128
SUBLANES = 8
VMEM_LIMIT = 56 * 1024 * 1024

TM_PROJ = 512
TQ = 256
TK = 128
TM_FFN = 512
FC = 768


def _softplus(x):
    return jnp.maximum(x, 0.0) + jnp.log1p(jnp.exp(-jnp.abs(x)))


def _log_sigmoid(x):
    return -_softplus(-x)


def _gelu(x):
    c = (2.0 / jnp.pi) ** 0.5
    return 0.5 * x * (1.0 + jnp.tanh(c * (x + 0.044715 * (x * x * x))))


def _layer_norm(x, g, b):
    mu = jnp.mean(x, axis=-1, keepdims=True)
    xc = x - mu
    var = jnp.mean(xc * xc, axis=-1, keepdims=True)
    return xc * lax.rsqrt(var + EPS) * g + b


def _rms_norm(x, g):
    return x * lax.rsqrt(jnp.mean(x * x, axis=-1, keepdims=True) + EPS) * g


def _dot(a, b):
    return jnp.dot(a, b, preferred_element_type=F32)


def _dot_nt(a, b):
    return lax.dot_general(a, b, (((1,), (1,)), ((), ())), preferred_element_type=F32)


def _prefix_sum_lanes(x, tri):
    hi = x.astype(BF16)
    r1 = x - hi.astype(F32)
    mid = r1.astype(BF16)
    lo = (r1 - mid.astype(F32)).astype(BF16)
    return _dot(hi, tri) + _dot(mid, tri) + _dot(lo, tri)


def _tri128():
    r = lax.broadcasted_iota(jnp.int32, (LANES, LANES), 0)
    c = lax.broadcasted_iota(jnp.int32, (LANES, LANES), 1)
    return (r <= c).astype(BF16)


def _running_sum_blocks(blocks, run):
    tri = _tri128()
    pre = [_prefix_sum_lanes(b, tri) for b in blocks]
    out = []
    for p in pre:
        out.append(p + run)
        run = run + p[:, LANES - 1:LANES]
    return out, run


def _split3(x):
    hi = x.astype(BF16).astype(F32)
    r1 = x - hi
    mid = r1.astype(BF16).astype(F32)
    lo = (r1 - mid).astype(BF16).astype(F32)
    return hi, mid, lo


def _rglru_gates(xc, wg_ref, bg_ref, lam_ref):
    g = _dot(xc.astype(BF16), wg_ref[...]) + bg_ref[...]
    r = jax.nn.sigmoid(g[:, :D_RNN])
    i = jax.nn.sigmoid(g[:, D_RNN:])
    log_a = (-RGLRU_C * _softplus(-lam_ref[...])) * r
    a = jnp.exp(log_a)
    mult = jnp.sqrt(-jnp.tanh(log_a) * (a * a + 1.0))
    return a, mult * (i * xc)


def _prompt_inproj_kernel(*refs, tm, tq, tk, n_alias):
    (x_ref, wn_ref, wt_ref, bfc_ref, bfr_ref, cw_ref, cb_ref, wg_ref, bg_ref, lam_ref,
     grnn_ref) = refs[:11]
    (qblk_ref, kT_ref, vT_ref, kblk_ref, vblk_ref, lfT_ref, yr_ref, hl_ref, cr_ref,
     xbuf, a_scr, u_scr, h_scr, hc_scr, crun_scr) = refs[11 + n_alias:]
    t = pl.program_id(1)

    @pl.when(t == 0)
    def _():
        xbuf[pl.ds(0, SUBLANES), :] = jnp.zeros((SUBLANES, D_RNN), F32)
        hc_scr[...] = jnp.zeros((SUBLANES, D_RNN), F32)
        crun_scr[...] = jnp.zeros(crun_scr.shape, F32)

    xb = x_ref[...].astype(BF16)
    yn = _dot(xb, wn_ref[...])
    yt = _dot_nt(wt_ref[...], xb)

    qT = yt[0:D_ATT] * ATT_SCALE
    kT = yt[D_ATT:2 * D_ATT]
    vT = yt[2 * D_ATT:3 * D_ATT]
    kT_ref[...] = kT
    vT_ref[...] = vT
    lfT_ref[...] = _log_sigmoid(yt[3 * D_ATT:3 * D_ATT + N_HEADS] + bfc_ref[...])

    lf_rows = _log_sigmoid(yn[:, D_ATT:D_ATT + LANES] + bfr_ref[...])
    ri = lax.broadcasted_iota(jnp.int32, (LANES, LANES), 0)
    ci = lax.broadcasted_iota(jnp.int32, (LANES, LANES), 1)
    lower = (ci <= ri).astype(BF16)
    run = crun_scr[0:1, :]
    c_rows = []
    for j in range(tm // LANES):
        hi, mid, lo = _split3(lf_rows[j * LANES:(j + 1) * LANES])
        pre = (_dot(lower, hi.astype(BF16)) + _dot(lower, mid.astype(BF16))
               + _dot(lower, lo.astype(BF16)))
        c_rows.append(pre + run)
        run = run + pre[LANES - 1:LANES, :]
    crun_scr[...] = jnp.broadcast_to(run, crun_scr.shape)
    nhi, nmid, nlo = _split3(-jnp.concatenate(c_rows, axis=0))

    lane = lax.broadcasted_iota(jnp.int32, (tm, LANES), 1)
    is_head = lane < N_HEADS
    packed = jnp.where(is_head, nhi, 0.0)
    packed = packed + pltpu.roll(jnp.where(is_head, nmid, 0.0), N_HEADS, 1)
    packed = packed + pltpu.roll(jnp.where(is_head, nlo, 0.0), 2 * N_HEADS, 1)
    src = lax.broadcasted_iota(jnp.int32, (LANES, N_HEADS * LANES), 0)
    dst = lax.broadcasted_iota(jnp.int32, (LANES, N_HEADS * LANES), 1)
    src_head = src % N_HEADS
    src_term = src // N_HEADS
    base = jnp.where(src_head % 2 == 0, HEAD_DIM, 0)
    place = ((dst // LANES == src_head) & (dst % LANES == base + src_term)
             & (src_term < 3)).astype(BF16)
    bias_lanes = _dot(packed.astype(BF16), place)

    ones = jnp.ones((HEAD_DIM, tk), BF16)
    rowi = lax.broadcasted_iota(jnp.int32, (HEAD_DIM, tq), 0)
    sel_rows = jnp.where(rowi < 3, 1.0, 0.0).astype(BF16)
    for h in range(N_HEADS):
        pair = yn[:, (h // 2) * LANES:(h // 2 + 1) * LANES]
        own = (lane < HEAD_DIM) if h % 2 == 0 else (lane >= HEAD_DIM)
        kb = (jnp.where(own, pair, 0.0) + bias_lanes[:, h * LANES:(h + 1) * LANES]).astype(BF16)
        chan = slice(0, HEAD_DIM) if h % 2 == 0 else slice(HEAD_DIM, 2 * HEAD_DIM)
        rest = slice(HEAD_DIM, 2 * HEAD_DIM) if h % 2 == 0 else slice(0, HEAD_DIM)
        hrows = slice(h * HEAD_DIM, (h + 1) * HEAD_DIM)
        for c in range(tm // tk):
            cols = slice(c * tk, (c + 1) * tk)
            kblk_ref[c, h] = kb[cols, :]
            vblk_ref[c, h, chan, :] = vT[hrows, cols].astype(BF16)
            vblk_ref[c, h, rest, :] = ones
        for c in range(tm // tq):
            cols = slice(c * tq, (c + 1) * tq)
            qblk_ref[c, h, chan, :] = qT[hrows, cols].astype(BF16)
            qblk_ref[c, h, rest, :] = sel_rows

    xr = yn[:, D_ATT + LANES:D_ATT + LANES + D_RNN]
    gr = yn[:, D_ATT + LANES + D_RNN:]
    xbuf[pl.ds(SUBLANES, tm), :] = xr
    xc = (cb_ref[...] + cw_ref[3:4, :] * xr
          + cw_ref[2:3, :] * xbuf[pl.ds(SUBLANES - 1, tm), :]
          + cw_ref[1:2, :] * xbuf[pl.ds(SUBLANES - 2, tm), :]
          + cw_ref[0:1, :] * xbuf[pl.ds(SUBLANES - 3, tm), :])
    tail = xbuf[pl.ds(tm, SUBLANES), :]
    xbuf[pl.ds(0, SUBLANES), :] = tail
    cr_ref[...] = tail[SUBLANES - 3:, :]

    a, u = _rglru_gates(xc, wg_ref, bg_ref, lam_ref)
    a_scr[...] = a
    u_scr[...] = u

    sub = lax.broadcasted_iota(jnp.int32, (SUBLANES, D_RNN), 0)

    def group(g, hin):
        r0 = pl.multiple_of(g * SUBLANES, SUBLANES)
        ag = a_scr[pl.ds(r0, SUBLANES), :]
        ug = u_scr[pl.ds(r0, SUBLANES), :]
        for d in (1, 2, 4):
            keep = sub >= d
            a_sh = pltpu.roll(ag, d, 0)
            u_sh = pltpu.roll(ug, d, 0)
            ug = jnp.where(keep, ag * u_sh + ug, ug)
            ag = jnp.where(keep, ag * a_sh, ag)
        hg = ag * hin + ug
        h_scr[pl.ds(r0, SUBLANES), :] = hg
        return jnp.broadcast_to(hg[SUBLANES - 1:SUBLANES, :], (SUBLANES, D_RNN))

    hin = lax.fori_loop(0, tm // SUBLANES, group, hc_scr[...], unroll=4)
    hc_scr[...] = hin
    hl_ref[...] = hin[0:1, :]

    y = h_scr[...] * _gelu(gr)
    yr_ref[...] = _rms_norm(y, grnn_ref[...]).astype(BF16)


def _prompt_inproj(layer, x, wn, wt, bf_col, bf_row, cw, cb, wg, bg, lam, grnn, stacked):
    B, T, _ = x.shape
    tm, tq, tk = TM_PROJ, TQ, TK
    nt = T // tm
    wt_rows = wt.shape[0]
    wn_cols = wn.shape[1]
    const = lambda b, t: (0, 0)
    n_alias = 0 if stacked is None else 3
    kern = functools.partial(_prompt_inproj_kernel, tm=tm, tq=tq, tk=tk, n_alias=n_alias)
    in_specs = [
        pl.BlockSpec((None, tm, D_MODEL), lambda b, t: (b, t, 0)),
        pl.BlockSpec((D_MODEL, wn_cols), const),
        pl.BlockSpec((wt_rows, D_MODEL), const),
        pl.BlockSpec((N_HEADS, 1), const),
        pl.BlockSpec((1, LANES), const),
        pl.BlockSpec((4, D_RNN), const),
        pl.BlockSpec((1, D_RNN), const),
        pl.BlockSpec((D_RNN, 2 * D_RNN), const),
        pl.BlockSpec((1, 2 * D_RNN), const),
        pl.BlockSpec((1, D_RNN), const),
        pl.BlockSpec((1, D_RNN), const),
    ] + [pl.BlockSpec(memory_space=pl.ANY)] * n_alias
    args = (x, wn, wt, bf_col, bf_row, cw, cb, wg, bg, lam, grnn)
    args += () if stacked is None else tuple(stacked)
    q_blk = (None, tm // tq, N_HEADS, 2 * HEAD_DIM, tq)
    v_blk = (None, tm // tk, N_HEADS, 2 * HEAD_DIM, tk)
    k_blk = (None, tm // tk, N_HEADS, tk, 2 * HEAD_DIM)
    blk_map = lambda b, t: (b, t, 0, 0, 0)
    return pl.pallas_call(
        kern,
        grid=(B, nt),
        in_specs=in_specs,
        out_specs=[
            pl.BlockSpec(q_blk, blk_map),
            pl.BlockSpec((None, None, D_ATT, tm), lambda b, t: (layer, b, 0, t)),
            pl.BlockSpec((None, None, D_ATT, tm), lambda b, t: (layer, b, 0, t)),
            pl.BlockSpec(k_blk, blk_map),
            pl.BlockSpec(v_blk, blk_map),
            pl.BlockSpec((None, None, N_HEADS, tm), lambda b, t: (layer, b, 0, t)),
            pl.BlockSpec((None, tm, D_RNN), lambda b, t: (b, t, 0)),
            pl.BlockSpec((None, 1, D_RNN), lambda b, t: (b, 0, 0)),
            pl.BlockSpec((None, 3, D_RNN), lambda b, t: (b, 0, 0)),
        ],
        out_shape=[
            jax.ShapeDtypeStruct((B, T // tq, N_HEADS, 2 * HEAD_DIM, tq), BF16),
            jax.ShapeDtypeStruct((DEPTH, B, D_ATT, T), F32),
            jax.ShapeDtypeStruct((DEPTH, B, D_ATT, T), F32),
            jax.ShapeDtypeStruct((B, T // tk, N_HEADS, tk, 2 * HEAD_DIM), BF16),
            jax.ShapeDtypeStruct((B, T // tk, N_HEADS, 2 * HEAD_DIM, tk), BF16),
            jax.ShapeDtypeStruct((DEPTH, B, N_HEADS, T), F32),
            jax.ShapeDtypeStruct((B, T, D_RNN), BF16),
            jax.ShapeDtypeStruct((B, 1, D_RNN), F32),
            jax.ShapeDtypeStruct((B, 3, D_RNN), F32),
        ],
        scratch_shapes=[
            pltpu.VMEM((tm + SUBLANES, D_RNN), F32),
            pltpu.VMEM((tm, D_RNN), F32),
            pltpu.VMEM((tm, D_RNN), F32),
            pltpu.VMEM((tm, D_RNN), F32),
            pltpu.VMEM((SUBLANES, D_RNN), F32),
            pltpu.VMEM((N_HEADS, LANES), F32),
        ],
        input_output_aliases={} if stacked is None else {11: 1, 12: 2, 13: 5},
        compiler_params=pltpu.CompilerParams(
            dimension_semantics=("arbitrary", "arbitrary"), vmem_limit_bytes=VMEM_LIMIT),
        name="prompt_inproj",
    )(*args)


def _prompt_attn_kernel(q_ref, k_ref, v_ref, g_ref, o_ref, m_scr, acc_scr, *, tq, tk):
    qi = pl.program_id(1)
    kpq = tq // tk
    key = lax.broadcasted_iota(jnp.int32, (tk, tq), 0)
    qry = lax.broadcasted_iota(jnp.int32, (tk, tq), 1)

    for h in range(N_HEADS):
        m_scr[h] = jnp.full((1, tq), NEG_BIG, F32)
        acc_scr[h] = jnp.zeros((LANES, tq), F32)

    def tile(kj, diag):
        for h in range(N_HEADS):
            s = _dot(k_ref[kj, h], q_ref[h])
            if diag is not None:
                s = jnp.where(key + diag * tk <= qry, s, NEG_BIG)
            m = m_scr[h]
            m_new = jnp.maximum(m, jnp.max(s, axis=0, keepdims=True))
            alpha = jnp.exp(m - m_new)
            p = jnp.exp(s - m_new).astype(BF16)
            acc_scr[h] = alpha * acc_scr[h] + _dot(v_ref[kj, h], p)
            m_scr[h] = m_new

    def full_tiles(j, carry):
        for d in range(kpq):
            tile(j * kpq + d, None)
        return carry

    lax.fori_loop(0, qi, full_tiles, 0)
    for d in range(kpq):
        tile(qi * kpq + d, d)

    halves = []
    for h in range(N_HEADS):
        acc = acc_scr[h]
        if h % 2 == 0:
            halves.append(acc[:HEAD_DIM] / acc[HEAD_DIM:])
        else:
            halves.append(acc[HEAD_DIM:] / acc[:HEAD_DIM])
    yT = jnp.concatenate(halves, axis=0)
    ms = jnp.mean(yT * yT, axis=0, keepdims=True)
    ynT = (yT * lax.rsqrt(ms + EPS) * g_ref[...]).astype(BF16)
    er = lax.broadcasted_iota(jnp.int32, (tq, tq), 0)
    ec = lax.broadcasted_iota(jnp.int32, (tq, tq), 1)
    o_ref[...] = _dot_nt((er == ec).astype(BF16), ynT).astype(BF16)


def _prompt_attn(qblk, kblk, vblk, g_col):
    B, nq, _, _, tq = qblk.shape
    nk, tk = kblk.shape[1], kblk.shape[3]
    T = nq * tq
    kern = functools.partial(_prompt_attn_kernel, tq=tq, tk=tk)
    return pl.pallas_call(
        kern,
        grid=(B, nq),
        in_specs=[
            pl.BlockSpec((None, None, N_HEADS, 2 * HEAD_DIM, tq), lambda b, i: (b, i, 0, 0, 0)),
            pl.BlockSpec((None, nk, N_HEADS, tk, 2 * HEAD_DIM), lambda b, i: (b, 0, 0, 0, 0)),
            pl.BlockSpec((None, nk, N_HEADS, 2 * HEAD_DIM, tk), lambda b, i: (b, 0, 0, 0, 0)),
            pl.BlockSpec((D_ATT, 1), lambda b, i: (0, 0)),
        ],
        out_specs=pl.BlockSpec((None, tq, D_ATT), lambda b, i: (b, i, 0)),
        out_shape=jax.ShapeDtypeStruct((B, T, D_ATT), BF16),
        scratch_shapes=[
            pltpu.VMEM((N_HEADS, 1, tq), F32),
            pltpu.VMEM((N_HEADS, LANES, tq), F32),
        ],
        compiler_params=pltpu.CompilerParams(
            dimension_semantics=("arbitrary", "arbitrary"), vmem_limit_bytes=VMEM_LIMIT),
        name="prompt_attn",
    )(qblk, kblk, vblk, g_col)


def _mix_head(x_ref, ya_ref, yr_ref, wo_ref, l1g_ref, l1b_ref, x1_scr, x1b_scr, acc_scr):
    mix = _dot(ya_ref[...], wo_ref[0:D_ATT, :]) + _dot(yr_ref[...], wo_ref[D_ATT:, :])
    x1 = _layer_norm(DN_ALPHA * x_ref[...] + mix, l1g_ref[...], l1b_ref[...])
    x1_scr[...] = x1
    x1b_scr[...] = x1.astype(BF16)
    acc_scr[...] = jnp.zeros(acc_scr.shape, F32)


def _prompt_mix_ffn_kernel(pt_ref, x_ref, ya_ref, yr_ref, wo_ref, l1g_ref, l1b_ref, wug_ref, wuv_ref,
                           cw_ref, cb_ref, wd_ref, l2g_ref, l2b_ref,
                           qs_ref, kns_ref, vns_ref, lfns_ref, gcol_ref, *rest,
                           tm, tiles_per_seq, nj, n_pages):
    k_refs = rest[:n_pages]
    v_refs = rest[n_pages:2 * n_pages]
    lf_refs = rest[2 * n_pages:3 * n_pages]
    (o_ref, cf_ref, yas_ref, x1_scr, x1b_scr, acc_scr, gbuf, carry_scr,
     qb_scr, m_scr, l_scr, sacc_scr, c_scr) = rest[3 * n_pages:]
    del pt_ref
    i = pl.program_id(0)
    j = pl.program_id(1)

    @pl.when(j == 0)
    def _():
        _mix_head(x_ref, ya_ref, yr_ref, wo_ref, l1g_ref, l1b_ref, x1_scr, x1b_scr, acc_scr)
        _sample_attn_init(qs_ref, qb_scr, m_scr, l_scr, sacc_scr, c_scr)

    x1b = x1b_scr[...]
    gate = _dot(x1b, wug_ref[...])
    val = _dot(x1b, wuv_ref[...])

    first = (i % tiles_per_seq) == 0
    prev = jnp.where(first, 0.0, carry_scr[j])
    gbuf[pl.ds(0, SUBLANES), :] = prev
    gbuf[pl.ds(SUBLANES, tm), :] = gate
    gc = (cb_ref[...] + cw_ref[2:3, :] * gate
          + cw_ref[1:2, :] * gbuf[pl.ds(SUBLANES - 1, tm), :]
          + cw_ref[0:1, :] * gbuf[pl.ds(SUBLANES - 2, tm), :])
    tail = gbuf[pl.ds(tm, SUBLANES), :]
    carry_scr[j] = tail
    cf_ref[...] = tail[SUBLANES - 2:, :]

    hid = (_gelu(gc) * val).astype(BF16)
    acc_scr[...] += _dot(hid, wd_ref[...])

    _sample_attn_pages(k_refs, v_refs, lf_refs, qb_scr, m_scr, l_scr, sacc_scr, c_scr)

    @pl.when(j == nj - 1)
    def _():
        o_ref[...] = _layer_norm(DN_ALPHA * x1_scr[...] + acc_scr[...], l2g_ref[...], l2b_ref[...])
        _sample_attn_finish(qs_ref, kns_ref, vns_ref, lfns_ref, gcol_ref, yas_ref,
                            m_scr, l_scr, sacc_scr, c_scr)


def _prompt_mix_ffn(layer, x, ya, yr, wo, l1g, l1b, wu, cw, cb, wd, l2g, l2b, seq,
                    page_table, q_col, k_col, v_col, lf_new, g_col, cache_kT, cache_vT, cache_lfT):
    M = x.shape[0]
    tm, fc = TM_FFN, FC
    nj = D_FF // fc
    tiles_per_seq = seq // tm
    n_s, n_pt = page_table.shape
    assert M // tm == n_s and n_pt % nj == 0, "one sample sequence per row tile, pages split over chunks"
    P = n_pt // nj
    kern = functools.partial(_prompt_mix_ffn_kernel, tm=tm, tiles_per_seq=tiles_per_seq, nj=nj,
                             n_pages=P)
    row = lambda i, j, pt: (i, 0)
    const = lambda i, j, pt: (0, 0)
    seq3 = lambda i, j, pt: (i, 0, 0)

    def page_spec(k, shape):
        nd = len(shape)
        return pl.BlockSpec((None, None) + shape,
                            lambda i, j, pt, k=k: (layer, pt[i, j * P + k]) + (0,) * nd)

    kv_shape = (N_HEADS, HEAD_DIM, PAGE_SIZE)
    in_specs = [
        pl.BlockSpec((tm, D_MODEL), row),
        pl.BlockSpec((tm, D_ATT), row),
        pl.BlockSpec((tm, D_RNN), row),
        pl.BlockSpec((D_MODEL, D_MODEL), const),
        pl.BlockSpec((1, D_MODEL), const),
        pl.BlockSpec((1, D_MODEL), const),
        pl.BlockSpec((D_MODEL, fc), lambda i, j, pt: (0, j)),
        pl.BlockSpec((D_MODEL, fc), lambda i, j, pt: (0, j + nj)),
        pl.BlockSpec((3, fc), lambda i, j, pt: (0, j)),
        pl.BlockSpec((1, fc), lambda i, j, pt: (0, j)),
        pl.BlockSpec((fc, D_MODEL), lambda i, j, pt: (j, 0)),
        pl.BlockSpec((1, D_MODEL), const),
        pl.BlockSpec((1, D_MODEL), const),
        pl.BlockSpec((None, D_ATT, 1), seq3),
        pl.BlockSpec((None, D_ATT, 1), seq3),
        pl.BlockSpec((None, D_ATT, 1), seq3),
        pl.BlockSpec((None, N_HEADS, 1), seq3),
        pl.BlockSpec((D_ATT, 1), const),
    ]
    in_specs += [page_spec(k, kv_shape) for k in range(P)]
    in_specs += [page_spec(k, kv_shape) for k in range(P)]
    in_specs += [page_spec(k, (N_HEADS, PAGE_SIZE)) for k in range(P)]
    grid_spec = pltpu.PrefetchScalarGridSpec(
        num_scalar_prefetch=1,
        grid=(M // tm, nj),
        in_specs=in_specs,
        out_specs=[
            pl.BlockSpec((tm, D_MODEL), row),
            pl.BlockSpec((None, 2, fc), lambda i, j, pt: (i, 0, j)),
            pl.BlockSpec((None, D_ATT, 1), seq3),
        ],
        scratch_shapes=[
            pltpu.VMEM((tm, D_MODEL), F32),
            pltpu.VMEM((tm, D_MODEL), BF16),
            pltpu.VMEM((tm, D_MODEL), F32),
            pltpu.VMEM((tm + SUBLANES, fc), F32),
            pltpu.VMEM((nj, SUBLANES, fc), F32),
            pltpu.VMEM((N_HEADS, HEAD_DIM, PAGE_SIZE), F32),
            pltpu.VMEM((N_HEADS, 1, 1), F32),
            pltpu.VMEM((N_HEADS, 1, PAGE_SIZE), F32),
            pltpu.VMEM((N_HEADS, HEAD_DIM, PAGE_SIZE), F32),
            pltpu.VMEM((N_HEADS, 1), F32),
        ],
    )
    return pl.pallas_call(
        kern,
        grid_spec=grid_spec,
        out_shape=[
            jax.ShapeDtypeStruct((M, D_MODEL), F32),
            jax.ShapeDtypeStruct((M // tm, 2, D_FF), F32),
            jax.ShapeDtypeStruct((n_s, D_ATT, 1), BF16),
        ],
        compiler_params=pltpu.CompilerParams(
            dimension_semantics=("arbitrary", "arbitrary"), vmem_limit_bytes=VMEM_LIMIT),
        name="prompt_mix_ffn",
    )(page_table, x, ya, yr, wo, l1g, l1b, wu, wu, cw, cb, wd, l2g, l2b,
      q_col, k_col, v_col, lf_new, g_col,
      *([cache_kT] * P), *([cache_vT] * P), *([cache_lfT] * P))


def _sample_mix_ffn_kernel(x_ref, ya_ref, yr_ref, wo_ref, l1g_ref, l1b_ref, wug_ref, wuv_ref, p0_ref,
                           p1_ref, cw_ref, cb_ref, wd_ref, l2g_ref, l2b_ref,
                           o_ref, gate_ref,
                           x1_scr, x1b_scr, acc_scr, *, nj):
    j = pl.program_id(0)

    @pl.when(j == 0)
    def _():
        _mix_head(x_ref, ya_ref, yr_ref, wo_ref, l1g_ref, l1b_ref, x1_scr, x1b_scr, acc_scr)

    x1b = x1b_scr[...]
    gate = _dot(x1b, wug_ref[...])
    val = _dot(x1b, wuv_ref[...])
    gate_ref[...] = gate
    gc = (cb_ref[...] + cw_ref[2:3, :] * gate + cw_ref[1:2, :] * p1_ref[...]
          + cw_ref[0:1, :] * p0_ref[...])
    hid = (_gelu(gc) * val).astype(BF16)
    acc_scr[...] += _dot(hid, wd_ref[...])

    @pl.when(j == nj - 1)
    def _():
        o_ref[...] = _layer_norm(DN_ALPHA * x1_scr[...] + acc_scr[...], l2g_ref[...], l2b_ref[...])


def _sample_mix_ffn(x, ya, yr, wo, l1g, l1b, wu, p0, p1, cw, cb, wd, l2g, l2b):
    n = x.shape[0]
    fc = FC
    nj = D_FF // fc
    kern = functools.partial(_sample_mix_ffn_kernel, nj=nj)
    const = lambda j: (0, 0)
    chunk = lambda j: (0, j)
    return pl.pallas_call(
        kern,
        grid=(nj,),
        in_specs=[
            pl.BlockSpec((n, D_MODEL), const),
            pl.BlockSpec((n, D_ATT), const),
            pl.BlockSpec((n, D_RNN), const),
            pl.BlockSpec((D_MODEL, D_MODEL), const),
            pl.BlockSpec((1, D_MODEL), const),
            pl.BlockSpec((1, D_MODEL), const),
            pl.BlockSpec((D_MODEL, fc), chunk),
            pl.BlockSpec((D_MODEL, fc), lambda j: (0, j + nj)),
            pl.BlockSpec((n, fc), chunk),
            pl.BlockSpec((n, fc), chunk),
            pl.BlockSpec((3, fc), chunk),
            pl.BlockSpec((1, fc), chunk),
            pl.BlockSpec((fc, D_MODEL), lambda j: (j, 0)),
            pl.BlockSpec((1, D_MODEL), const),
            pl.BlockSpec((1, D_MODEL), const),
        ],
        out_specs=[
            pl.BlockSpec((n, D_MODEL), const),
            pl.BlockSpec((n, fc), chunk),
        ],
        out_shape=[
            jax.ShapeDtypeStruct((n, D_MODEL), F32),
            jax.ShapeDtypeStruct((n, D_FF), F32),
        ],
        scratch_shapes=[
            pltpu.VMEM((n, D_MODEL), F32),
            pltpu.VMEM((n, D_MODEL), BF16),
            pltpu.VMEM((n, D_MODEL), F32),
        ],
        compiler_params=pltpu.CompilerParams(
            dimension_semantics=("arbitrary",), vmem_limit_bytes=VMEM_LIMIT),
        name="sample_mix_ffn",
    )(x, ya, yr, wo, l1g, l1b, wu, wu, p0, p1, cw, cb, wd, l2g, l2b)


def _sample_inproj_kernel(x_ref, wn_ref, wt_ref, bf_ref, cs_ref, h0_ref, cw_ref, cb_ref, wg_ref, bg_ref,
                          lam_ref, grnn_ref,
                          q_ref, k_ref, v_ref, lf_ref, yr_ref, h_ref, csn_ref):
    xb = x_ref[...].astype(BF16)
    yn = _dot(xb, wn_ref[...])
    yt = _dot_nt(xb, wt_ref[...])
    q_ref[...] = yt[:, 0:D_ATT]
    k_ref[...] = yt[:, D_ATT:2 * D_ATT]
    v_ref[...] = yt[:, 2 * D_ATT:3 * D_ATT]
    lf_ref[...] = _log_sigmoid(yt[:, 3 * D_ATT:3 * D_ATT + N_HEADS] + bf_ref[...])

    xr = yn[:, D_ATT + LANES:D_ATT + LANES + D_RNN]
    gr = yn[:, D_ATT + LANES + D_RNN:]
    xc = (cb_ref[...] + cw_ref[3:4, :] * xr + cw_ref[2:3, :] * cs_ref[2]
          + cw_ref[1:2, :] * cs_ref[1] + cw_ref[0:1, :] * cs_ref[0])
    csn_ref[0] = cs_ref[1]
    csn_ref[1] = cs_ref[2]
    csn_ref[2] = xr

    a, u = _rglru_gates(xc, wg_ref, bg_ref, lam_ref)
    h = a * h0_ref[...] + u
    h_ref[...] = h
    yr_ref[...] = _rms_norm(h * _gelu(gr), grnn_ref[...]).astype(BF16)


def _sample_inproj(x, wn, wt, bf_row, cs, h0, cw, cb, wg, bg, lam, grnn):
    n = x.shape[0]
    return pl.pallas_call(
        _sample_inproj_kernel,
        out_shape=[
            jax.ShapeDtypeStruct((n, D_ATT), F32),
            jax.ShapeDtypeStruct((n, D_ATT), F32),
            jax.ShapeDtypeStruct((n, D_ATT), F32),
            jax.ShapeDtypeStruct((n, N_HEADS), F32),
            jax.ShapeDtypeStruct((n, D_RNN), BF16),
            jax.ShapeDtypeStruct((n, D_RNN), F32),
            jax.ShapeDtypeStruct((3, n, D_RNN), F32),
        ],
        compiler_params=pltpu.CompilerParams(vmem_limit_bytes=VMEM_LIMIT),
        name="sample_inproj",
    )(x, wn, wt, bf_row, cs, h0, cw, cb, wg, bg, lam, grnn)


def _sample_attn_init(q_ref, qb_scr, m_scr, l_scr, acc_scr, c_scr):
    qb_scr[...] = jnp.broadcast_to(q_ref[...].reshape(N_HEADS, HEAD_DIM, 1) * ATT_SCALE,
                                   (N_HEADS, HEAD_DIM, PAGE_SIZE))
    m_scr[...] = jnp.full(m_scr.shape, NEG_BIG, F32)
    l_scr[...] = jnp.zeros(l_scr.shape, F32)
    acc_scr[...] = jnp.zeros(acc_scr.shape, F32)
    c_scr[...] = jnp.zeros(c_scr.shape, F32)


def _sample_attn_pages(k_refs, v_refs, lf_refs, qb_scr, m_scr, l_scr, acc_scr, c_scr):
    n_pages = len(k_refs)
    cs, run = _running_sum_blocks([lf_refs[i][...] for i in range(n_pages)], c_scr[...])
    c_scr[...] = run
    qb = qb_scr[...]
    scores = []
    for i in range(n_pages):
        s_i = jnp.sum(k_refs[i][...] * qb, axis=1, keepdims=True)
        scores.append(s_i - cs[i].reshape(N_HEADS, 1, PAGE_SIZE))
    s = jnp.concatenate(scores, axis=-1)

    m = m_scr[...]
    m_new = jnp.maximum(m, jnp.max(s, axis=-1, keepdims=True))
    alpha = jnp.exp(m - m_new)
    p = jnp.exp(s - m_new)
    l_new = alpha * l_scr[...]
    acc = alpha * acc_scr[...]
    for i in range(n_pages):
        p_i = p[:, :, i * PAGE_SIZE:(i + 1) * PAGE_SIZE]
        l_new = l_new + p_i
        acc = acc + v_refs[i][...] * p_i
    l_scr[...] = l_new
    acc_scr[...] = acc
    m_scr[...] = m_new


def _sample_attn_finish(q_ref, kn_ref, vn_ref, lfn_ref, g_ref, o_ref, m_scr, l_scr, acc_scr, c_scr):
    hds = (N_HEADS, HEAD_DIM, 1)
    c_new = (c_scr[...] + lfn_ref[...]).reshape(N_HEADS, 1, 1)
    qc = q_ref[...].reshape(hds) * ATT_SCALE
    s_new = jnp.sum(qc * kn_ref[...].reshape(hds), axis=1, keepdims=True) - c_new
    m_old = m_scr[...]
    m_fin = jnp.maximum(m_old, s_new)
    a2 = jnp.exp(m_old - m_fin)
    p_new = jnp.exp(s_new - m_fin)
    l_fin = a2 * jnp.sum(l_scr[...], axis=-1, keepdims=True) + p_new
    o3 = (a2 * jnp.sum(acc_scr[...], axis=-1, keepdims=True)
          + p_new * vn_ref[...].reshape(hds)) / l_fin
    y = o3.reshape(D_ATT, 1)
    ms = jnp.mean(y * y, axis=0, keepdims=True)
    o_ref[...] = (y * lax.rsqrt(ms + EPS) * g_ref[...]).astype(BF16)


def _block_diag(w):
    n, c, d = w.shape
    return jnp.einsum("ncd,nm->ncmd", w, jnp.eye(n, dtype=w.dtype)).reshape(n * c, n * d)


def kernel(x_prompt, x_sample, cache_k, cache_v, cache_logf, page_table, state_h, state_conv_rnn,
           state_conv_ffn, w_in, b_f, rnn_conv_w, rnn_conv_b, w_a, b_a, w_x, b_x, lam, g_att, g_rnn,
           w_out, ln1_g, ln1_b, w_up, ffn_conv_w, ffn_conv_b, w_down, ln2_g, ln2_b):
    B, T, _ = x_prompt.shape
    n_s = x_sample.shape[0]

    cache_kT = jnp.transpose(cache_k, (0, 1, 3, 4, 2))
    cache_vT = jnp.transpose(cache_v, (0, 1, 3, 4, 2))
    cache_lfT = jnp.transpose(cache_logf, (0, 1, 3, 2))

    x_p = x_prompt
    x_s = x_sample.reshape(n_s, D_MODEL)
    outs = [[] for _ in range(12)]
    stacked = None
    c1, c2, c3 = D_ATT, 2 * D_ATT, 3 * D_ATT
    c4 = c3 + N_HEADS
    c5 = c4 + D_RNN
    for l in range(DEPTH):
        wi = w_in[l]
        wn = jnp.concatenate([wi[:, c1:c2], wi[:, c3:c4], jnp.zeros((D_MODEL, LANES - N_HEADS), F32),
                              wi[:, c4:c5], wi[:, c5:]], axis=1).astype(BF16)
        wt = jnp.concatenate([wi[:, :c4], jnp.zeros((D_MODEL, 8), F32)], axis=1).T.astype(BF16)
        bf_row = jnp.concatenate([b_f[l], jnp.zeros((LANES - N_HEADS,), F32)])[None, :]
        wg = jnp.concatenate([_block_diag(w_a[l]), _block_diag(w_x[l])], axis=1).astype(BF16)
        bg = jnp.concatenate([b_a[l], b_x[l]])[None, :]
        wo = w_out[l].astype(BF16)
        wu = w_up[l].astype(BF16)
        wd = w_down[l].astype(BF16)
        row = lambda v: v[None, :]
        cw, cb = rnn_conv_w[l], row(rnn_conv_b[l])
        lam_l, grnn, g_col = row(lam[l]), row(g_rnn[l]), g_att[l][:, None]
        l1g, l1b, l2g, l2b = row(ln1_g[l]), row(ln1_b[l]), row(ln2_g[l]), row(ln2_b[l])
        fcw, fcb = ffn_conv_w[l], row(ffn_conv_b[l])

        cs = jnp.transpose(state_conv_rnn[l], (1, 0, 2))
        qs, ks, vs, lfs, yrs, hs, csn = _sample_inproj(
            x_s, wn, wt, b_f[l][None, :], cs, state_h[l], cw, cb, wg, bg, lam_l, grnn)

        qblk, kT_all, vT_all, kblk, vblk, lfT_all, yr, h_last, cr = _prompt_inproj(
            l, x_p, wn, wt, b_f[l][:, None], bf_row, cw, cb, wg, bg, lam_l, grnn, stacked)
        stacked = (kT_all, vT_all, lfT_all)
        ya = _prompt_attn(qblk, kblk, vblk, g_col)
        x2, cf, yas = _prompt_mix_ffn(
            l, x_p.reshape(B * T, D_MODEL), ya.reshape(B * T, D_ATT), yr.reshape(B * T, D_RNN),
            wo, l1g, l1b, wu, fcw, fcb, wd, l2g, l2b, T,
            page_table, qs.reshape(n_s, D_ATT, 1), ks.reshape(n_s, D_ATT, 1),
            vs.reshape(n_s, D_ATT, 1), lfs.reshape(n_s, N_HEADS, 1), g_col,
            cache_kT, cache_vT, cache_lfT)
        x_p = x2.reshape(B, T, D_MODEL)
        outs[3].append(h_last.reshape(B, D_RNN))
        outs[4].append(cr)
        tiles_per_seq = T // TM_FFN
        outs[5].append(cf[tiles_per_seq - 1::tiles_per_seq])

        p0 = state_conv_ffn[l][:, 0, :]
        p1 = state_conv_ffn[l][:, 1, :]
        x_s, gate_s = _sample_mix_ffn(x_s, yas.reshape(n_s, D_ATT), yrs, wo, l1g, l1b, wu, p0, p1,
                                      fcw, fcb, wd, l2g, l2b)
        outs[6].append(ks.reshape(n_s, 1, N_HEADS, HEAD_DIM))
        outs[7].append(vs.reshape(n_s, 1, N_HEADS, HEAD_DIM))
        outs[8].append(lfs.reshape(n_s, 1, N_HEADS))
        outs[9].append(hs)
        outs[10].append(jnp.transpose(csn, (1, 0, 2)))
        outs[11].append(jnp.stack([p1, gate_s], axis=1))

    st = [jnp.stack(o) for o in outs[3:]]
    kT_all, vT_all, lfT_all = stacked
    k_p = kT_all.reshape(DEPTH, B, N_HEADS, HEAD_DIM, T).transpose(0, 1, 4, 2, 3)
    v_p = vT_all.reshape(DEPTH, B, N_HEADS, HEAD_DIM, T).transpose(0, 1, 4, 2, 3)
    lf_p = lfT_all.transpose(0, 1, 3, 2)
    return (x_p, x_s.reshape(n_s, 1, D_MODEL), k_p, v_p, lf_p, *st)
```

```python
import functools

import jax
import jax.numpy as jnp
from jax import lax
from jax.experimental import pallas as pl
from jax.experimental.pallas import tpu as pltpu

F32 = jnp.float32
BF16 = jnp.bfloat16

D_MODEL = 1024
D_ATT = 512
D_RNN = 512
N_HEADS = 8
HEAD_DIM = 64
N_RNN_BLOCKS = 8
D_FF = 3072
RGLRU_C = 8.0
DEPTH = 2
PAGE_SIZE = 128
DN_ALPHA = (2 * DEPTH) ** 0.25
EPS = 1e-5
ATT_SCALE = HEAD_DIM ** -0.5
NEG_BIG = -1e30

LANES = 128
SUBLANES = 8
VMEM_LIMIT = 56 * 1024 * 1024

TM_PROJ = 512
TQ = 256
TK = 128
TM_FFN = 512
FC = 768


def _softplus(x):
    return jnp.maximum(x, 0.0) + jnp.log1p(jnp.exp(-jnp.abs(x)))


def _log_sigmoid(x):
    return -_softplus(-x)


def _gelu(x):
    c = (2.0 / jnp.pi) ** 0.5
    return 0.5 * x * (1.0 + jnp.tanh(c * (x + 0.044715 * (x * x * x))))


def _layer_norm(x, g, b):
    mu = jnp.mean(x, axis=-1, keepdims=True)
    xc = x - mu
    var = jnp.mean(xc * xc, axis=-1, keepdims=True)
    return xc * lax.rsqrt(var + EPS) * g + b


def _rms_norm(x, g):
    return x * lax.rsqrt(jnp.mean(x * x, axis=-1, keepdims=True) + EPS) * g


def _dot(a, b):
    return jnp.dot(a, b, preferred_element_type=F32)


def _dot_nt(a, b):
    return lax.dot_general(a, b, (((1,), (1,)), ((), ())), preferred_element_type=F32)


def _prefix_sum_lanes(x, tri):
    hi = x.astype(BF16)
    r1 = x - hi.astype(F32)
    mid = r1.astype(BF16)
    lo = (r1 - mid.astype(F32)).astype(BF16)
    return _dot(hi, tri) + _dot(mid, tri) + _dot(lo, tri)


def _tri128():
    r = lax.broadcasted_iota(jnp.int32, (LANES, LANES), 0)
    c = lax.broadcasted_iota(jnp.int32, (LANES, LANES), 1)
    return (r <= c).astype(BF16)


def _running_sum_blocks(blocks, run):
    tri = _tri128()
    pre = [_prefix_sum_lanes(b, tri) for b in blocks]
    out = []
    for p in pre:
        out.append(p + run)
        run = run + p[:, LANES - 1:LANES]
    return out, run


def _split3(x):
    hi = x.astype(BF16).astype(F32)
    r1 = x - hi
    mid = r1.astype(BF16).astype(F32)
    lo = (r1 - mid).astype(BF16).astype(F32)
    return hi, mid, lo


def _rglru_gates(xc, wg_ref, bg_ref, lam_ref):
    g = _dot(xc.astype(BF16), wg_ref[...]) + bg_ref[...]
    r = jax.nn.sigmoid(g[:, :D_RNN])
    i = jax.nn.sigmoid(g[:, D_RNN:])
    log_a = (-RGLRU_C * _softplus(-lam_ref[...])) * r
    a = jnp.exp(log_a)
    mult = jnp.sqrt(-jnp.tanh(log_a) * (a * a + 1.0))
    return a, mult * (i * xc)


def _prompt_inproj_kernel(*refs, tm, tq, tk, n_alias):
    (x_ref, wn_ref, wt_ref, bfc_ref, bfr_ref, cw_ref, cb_ref, wg_ref, bg_ref, lam_ref,
     grnn_ref) = refs[:11]
    (qblk_ref, kT_ref, vT_ref, kblk_ref, vblk_ref, lfT_ref, yr_ref, hl_ref, cr_ref,
     xbuf, a_scr, u_scr, h_scr, hc_scr, crun_scr) = refs[11 + n_alias:]
    t = pl.program_id(1)

    @pl.when(t == 0)
    def _():
        xbuf[pl.ds(0, SUBLANES), :] = jnp.zeros((SUBLANES, D_RNN), F32)
        hc_scr[...] = jnp.zeros((SUBLANES, D_RNN), F32)
        crun_scr[...] = jnp.zeros(crun_scr.shape, F32)

    xb = x_ref[...].astype(BF16)
    yn = _dot(xb, wn_ref[...])
    yt = _dot_nt(wt_ref[...], xb)

    qT = yt[0:D_ATT] * ATT_SCALE
    kT = yt[D_ATT:2 * D_ATT]
    vT = yt[2 * D_ATT:3 * D_ATT]
    kT_ref[...] = kT
    vT_ref[...] = vT
    lfT_ref[...] = _log_sigmoid(yt[3 * D_ATT:3 * D_ATT + N_HEADS] + bfc_ref[...])

    lf_rows = _log_sigmoid(yn[:, D_ATT:D_ATT + LANES] + bfr_ref[...])
    ri = lax.broadcasted_iota(jnp.int32, (LANES, LANES), 0)
    ci = lax.broadcasted_iota(jnp.int32, (LANES, LANES), 1)
    lower = (ci <= ri).astype(BF16)
    run = crun_scr[0:1, :]
    c_rows = []
    for j in range(tm // LANES):
        hi, mid, lo = _split3(lf_rows[j * LANES:(j + 1) * LANES])
        pre = (_dot(lower, hi.astype(BF16)) + _dot(lower, mid.astype(BF16))
               + _dot(lower, lo.astype(BF16)))
        c_rows.append(pre + run)
        run = run + pre[LANES - 1:LANES, :]
    crun_scr[...] = jnp.broadcast_to(run, crun_scr.shape)
    nhi, nmid, nlo = _split3(-jnp.concatenate(c_rows, axis=0))

    lane = lax.broadcasted_iota(jnp.int32, (tm, LANES), 1)
    is_head = lane < N_HEADS
    packed = jnp.where(is_head, nhi, 0.0)
    packed = packed + pltpu.roll(jnp.where(is_head, nmid, 0.0), N_HEADS, 1)
    packed = packed + pltpu.roll(jnp.where(is_head, nlo, 0.0), 2 * N_HEADS, 1)
    src = lax.broadcasted_iota(jnp.int32, (LANES, N_HEADS * LANES), 0)
    dst = lax.broadcasted_iota(jnp.int32, (LANES, N_HEADS * LANES), 1)
    src_head = src % N_HEADS
    src_term = src // N_HEADS
    base = jnp.where(src_head % 2 == 0, HEAD_DIM, 0)
    place = ((dst // LANES == src_head) & (dst % LANES == base + src_term)
             & (src_term < 3)).astype(BF16)
    bias_lanes = _dot(packed.astype(BF16), place)

    ones = jnp.ones((HEAD_DIM, tk), BF16)
    rowi = lax.broadcasted_iota(jnp.int32, (HEAD_DIM, tq), 0)
    sel_rows = jnp.where(rowi < 3, 1.0, 0.0).astype(BF16)
    for h in range(N_HEADS):
        pair = yn[:, (h // 2) * LANES:(h // 2 + 1) * LANES]
        own = (lane < HEAD_DIM) if h % 2 == 0 else (lane >= HEAD_DIM)
        kb = (jnp.where(own, pair, 0.0) + bias_lanes[:, h * LANES:(h + 1) * LANES]).astype(BF16)
        chan = slice(0, HEAD_DIM) if h % 2 == 0 else slice(HEAD_DIM, 2 * HEAD_DIM)
        rest = slice(HEAD_DIM, 2 * HEAD_DIM) if h % 2 == 0 else slice(0, HEAD_DIM)
        hrows = slice(h * HEAD_DIM, (h + 1) * HEAD_DIM)
        for c in range(tm // tk):
            cols = slice(c * tk, (c + 1) * tk)
            kblk_ref[c, h] = kb[cols, :]
            vblk_ref[c, h, chan, :] = vT[hrows, cols].astype(BF16)
            vblk_ref[c, h, rest, :] = ones
        for c in range(tm // tq):
            cols = slice(c * tq, (c + 1) * tq)
            qblk_ref[c, h, chan, :] = qT[hrows, cols].astype(BF16)
            qblk_ref[c, h, rest, :] = sel_rows

    xr = yn[:, D_ATT + LANES:D_ATT + LANES + D_RNN]
    gr = yn[:, D_ATT + LANES + D_RNN:]
    xbuf[pl.ds(SUBLANES, tm), :] = xr
    xc = (cb_ref[...] + cw_ref[3:4, :] * xr
          + cw_ref[2:3, :] * xbuf[pl.ds(SUBLANES - 1, tm), :]
          + cw_ref[1:2, :] * xbuf[pl.ds(SUBLANES - 2, tm), :]
          + cw_ref[0:1, :] * xbuf[pl.ds(SUBLANES - 3, tm), :])
    tail = xbuf[pl.ds(tm, SUBLANES), :]
    xbuf[pl.ds(0, SUBLANES), :] = tail
    cr_ref[...] = tail[SUBLANES - 3:, :]

    a, u = _rglru_gates(xc, wg_ref, bg_ref, lam_ref)
    a_scr[...] = a
    u_scr[...] = u

    sub = lax.broadcasted_iota(jnp.int32, (SUBLANES, D_RNN), 0)

    def group(g, hin):
        r0 = g * SUBLANES
        ag = a_scr[pl.ds(r0, SUBLANES), :]
        ug = u_scr[pl.ds(r0, SUBLANES), :]
        for d in (1, 2, 4):
            keep = sub >= d
            a_sh = pltpu.roll(ag, d, 0)
            u_sh = pltpu.roll(ug, d, 0)
            ug = jnp.where(keep, ag * u_sh + ug, ug)
            ag = jnp.where(keep, ag * a_sh, ag)
        hg = ag * hin + ug
        h_scr[pl.ds(r0, SUBLANES), :] = hg
        return jnp.broadcast_to(hg[SUBLANES - 1:SUBLANES, :], (SUBLANES, D_RNN))

    hin = hc_scr[...]
    for g in range(tm // SUBLANES):
        hin = group(g, hin)
    hc_scr[...] = hin
    hl_ref[...] = hin[0:1, :]

    y = h_scr[...] * _gelu(gr)
    yr_ref[...] = _rms_norm(y, grnn_ref[...]).astype(BF16)


def _prompt_inproj(layer, x, wn, wt, bf_col, bf_row, cw, cb, wg, bg, lam, grnn, stacked):
    B, T, _ = x.shape
    tm, tq, tk = TM_PROJ, TQ, TK
    nt = T // tm
    wt_rows = wt.shape[0]
    wn_cols = wn.shape[1]
    const = lambda b, t: (0, 0)
    n_alias = 0 if stacked is None else 3
    kern = functools.partial(_prompt_inproj_kernel, tm=tm, tq=tq, tk=tk, n_alias=n_alias)
    in_specs = [
        pl.BlockSpec((None, tm, D_MODEL), lambda b, t: (b, t, 0)),
        pl.BlockSpec((D_MODEL, wn_cols), const),
        pl.BlockSpec((wt_rows, D_MODEL), const),
        pl.BlockSpec((N_HEADS, 1), const),
        pl.BlockSpec((1, LANES), const),
        pl.BlockSpec((4, D_RNN), const),
        pl.BlockSpec((1, D_RNN), const),
        pl.BlockSpec((D_RNN, 2 * D_RNN), const),
        pl.BlockSpec((1, 2 * D_RNN), const),
        pl.BlockSpec((1, D_RNN), const),
        pl.BlockSpec((1, D_RNN), const),
    ] + [pl.BlockSpec(memory_space=pl.ANY)] * n_alias
    args = (x, wn, wt, bf_col, bf_row, cw, cb, wg, bg, lam, grnn)
    args += () if stacked is None else tuple(stacked)
    q_blk = (None, tm // tq, N_HEADS, 2 * HEAD_DIM, tq)
    v_blk = (None, tm // tk, N_HEADS, 2 * HEAD_DIM, tk)
    k_blk = (None, tm // tk, N_HEADS, tk, 2 * HEAD_DIM)
    blk_map = lambda b, t: (b, t, 0, 0, 0)
    return pl.pallas_call(
        kern,
        grid=(B, nt),
        in_specs=in_specs,
        out_specs=[
            pl.BlockSpec(q_blk, blk_map),
            pl.BlockSpec((None, None, D_ATT, tm), lambda b, t: (layer, b, 0, t)),
            pl.BlockSpec((None, None, D_ATT, tm), lambda b, t: (layer, b, 0, t)),
            pl.BlockSpec(k_blk, blk_map),
            pl.BlockSpec(v_blk, blk_map),
            pl.BlockSpec((None, None, N_HEADS, tm), lambda b, t: (layer, b, 0, t)),
            pl.BlockSpec((None, tm, D_RNN), lambda b, t: (b, t, 0)),
            pl.BlockSpec((None, 1, D_RNN), lambda b, t: (b, 0, 0)),
            pl.BlockSpec((None, 3, D_RNN), lambda b, t: (b, 0, 0)),
        ],
        out_shape=[
            jax.ShapeDtypeStruct((B, T // tq, N_HEADS, 2 * HEAD_DIM, tq), BF16),
            jax.ShapeDtypeStruct((DEPTH, B, D_ATT, T), F32),
            jax.ShapeDtypeStruct((DEPTH, B, D_ATT, T), F32),
            jax.ShapeDtypeStruct((B, T // tk, N_HEADS, tk, 2 * HEAD_DIM), BF16),
            jax.ShapeDtypeStruct((B, T // tk, N_HEADS, 2 * HEAD_DIM, tk), BF16),
            jax.ShapeDtypeStruct((DEPTH, B, N_HEADS, T), F32),
            jax.ShapeDtypeStruct((B, T, D_RNN), BF16),
            jax.ShapeDtypeStruct((B, 1, D_RNN), F32),
            jax.ShapeDtypeStruct((B, 3, D_RNN), F32),
        ],
        scratch_shapes=[
            pltpu.VMEM((tm + SUBLANES, D_RNN), F32),
            pltpu.VMEM((tm, D_RNN), F32),
            pltpu.VMEM((tm, D_RNN), F32),
            pltpu.VMEM((tm, D_RNN), F32),
            pltpu.VMEM((SUBLANES, D_RNN), F32),
            pltpu.VMEM((N_HEADS, LANES), F32),
        ],
        input_output_aliases={} if stacked is None else {11: 1, 12: 2, 13: 5},
        compiler_params=pltpu.CompilerParams(
            dimension_semantics=("arbitrary", "arbitrary"), vmem_limit_bytes=VMEM_LIMIT),
        name="prompt_inproj",
    )(*args)


def _prompt_attn_kernel(q_ref, k_ref, v_ref, g_ref, o_ref, m_scr, acc_scr, *, tq, tk):
    qi = pl.program_id(1)
    kpq = tq // tk
    key = lax.broadcasted_iota(jnp.int32, (tk, tq), 0)
    qry = lax.broadcasted_iota(jnp.int32, (tk, tq), 1)

    for h in range(N_HEADS):
        m_scr[h] = jnp.full((1, tq), NEG_BIG, F32)
        acc_scr[h] = jnp.zeros((LANES, tq), F32)

    def tile(kj, diag):
        for h in range(N_HEADS):
            s = _dot(k_ref[kj, h], q_ref[h])
            if diag is not None:
                s = jnp.where(key + diag * tk <= qry, s, NEG_BIG)
            m = m_scr[h]
            m_new = jnp.maximum(m, jnp.max(s, axis=0, keepdims=True))
            alpha = jnp.exp(m - m_new)
            p = jnp.exp(s - m_new).astype(BF16)
            acc_scr[h] = alpha * acc_scr[h] + _dot(v_ref[kj, h], p)
            m_scr[h] = m_new

    def full_tiles(j, carry):
        for d in range(kpq):
            tile(j * kpq + d, None)
        return carry

    lax.fori_loop(0, qi, full_tiles, 0)
    for d in range(kpq):
        tile(qi * kpq + d, d)

    halves = []
    for h in range(N_HEADS):
        acc = acc_scr[h]
        if h % 2 == 0:
            halves.append(acc[:HEAD_DIM] / acc[HEAD_DIM:])
        else:
            halves.append(acc[HEAD_DIM:] / acc[:HEAD_DIM])
    yT = jnp.concatenate(halves, axis=0)
    ms = jnp.mean(yT * yT, axis=0, keepdims=True)
    ynT = (yT * lax.rsqrt(ms + EPS) * g_ref[...]).astype(BF16)
    er = lax.broadcasted_iota(jnp.int32, (tq, tq), 0)
    ec = lax.broadcasted_iota(jnp.int32, (tq, tq), 1)
    o_ref[...] = _dot_nt((er == ec).astype(BF16), ynT).astype(BF16)


def _prompt_attn(qblk, kblk, vblk, g_col):
    B, nq, _, _, tq = qblk.shape
    nk, tk = kblk.shape[1], kblk.shape[3]
    T = nq * tq
    kern = functools.partial(_prompt_attn_kernel, tq=tq, tk=tk)
    return pl.pallas_call(
        kern,
        grid=(B, nq),
        in_specs=[
            pl.BlockSpec((None, None, N_HEADS, 2 * HEAD_DIM, tq), lambda b, i: (b, i, 0, 0, 0)),
            pl.BlockSpec((None, nk, N_HEADS, tk, 2 * HEAD_DIM), lambda b, i: (b, 0, 0, 0, 0)),
            pl.BlockSpec((None, nk, N_HEADS, 2 * HEAD_DIM, tk), lambda b, i: (b, 0, 0, 0, 0)),
            pl.BlockSpec((D_ATT, 1), lambda b, i: (0, 0)),
        ],
        out_specs=pl.BlockSpec((None, tq, D_ATT), lambda b, i: (b, i, 0)),
        out_shape=jax.ShapeDtypeStruct((B, T, D_ATT), BF16),
        scratch_shapes=[
            pltpu.VMEM((N_HEADS, 1, tq), F32),
            pltpu.VMEM((N_HEADS, LANES, tq), F32),
        ],
        compiler_params=pltpu.CompilerParams(
            dimension_semantics=("arbitrary", "arbitrary"), vmem_limit_bytes=VMEM_LIMIT),
        name="prompt_attn",
    )(qblk, kblk, vblk, g_col)


def _mix_head(x_ref, ya_ref, yr_ref, wo_ref, l1g_ref, l1b_ref, x1_scr, x1b_scr, acc_scr):
    mix = _dot(ya_ref[...], wo_ref[0:D_ATT, :]) + _dot(yr_ref[...], wo_ref[D_ATT:, :])
    x1 = _layer_norm(DN_ALPHA * x_ref[...] + mix, l1g_ref[...], l1b_ref[...])
    x1_scr[...] = x1
    x1b_scr[...] = x1.astype(BF16)
    acc_scr[...] = jnp.zeros(acc_scr.shape, F32)


def _prompt_mix_ffn_kernel(pt_ref, x_ref, ya_ref, yr_ref, wo_ref, l1g_ref, l1b_ref, wug_ref, wuv_ref,
                           cw_ref, cb_ref, wd_ref, l2g_ref, l2b_ref,
                           qs_ref, kns_ref, vns_ref, lfns_ref, gcol_ref, *rest,
                           tm, tiles_per_seq, nj, n_pages):
    k_refs = rest[:n_pages]
    v_refs = rest[n_pages:2 * n_pages]
    lf_refs = rest[2 * n_pages:3 * n_pages]
    (o_ref, cf_ref, yas_ref, x1_scr, x1b_scr, acc_scr, gbuf, carry_scr,
     qb_scr, m_scr, l_scr, sacc_scr, c_scr) = rest[3 * n_pages:]
    del pt_ref
    i = pl.program_id(0)
    j = pl.program_id(1)

    @pl.when(j == 0)
    def _():
        _mix_head(x_ref, ya_ref, yr_ref, wo_ref, l1g_ref, l1b_ref, x1_scr, x1b_scr, acc_scr)
        _sample_attn_init(qs_ref, qb_scr, m_scr, l_scr, sacc_scr, c_scr)

    _sample_attn_pages(k_refs, v_refs, lf_refs, qb_scr, m_scr, l_scr, sacc_scr, c_scr)

    x1b = x1b_scr[...]
    gate = _dot(x1b, wug_ref[...])
    val = _dot(x1b, wuv_ref[...])

    first = (i % tiles_per_seq) == 0
    prev = jnp.where(first, 0.0, carry_scr[j])
    gbuf[pl.ds(0, SUBLANES), :] = prev
    gbuf[pl.ds(SUBLANES, tm), :] = gate
    gc = (cb_ref[...] + cw_ref[2:3, :] * gate
          + cw_ref[1:2, :] * gbuf[pl.ds(SUBLANES - 1, tm), :]
          + cw_ref[0:1, :] * gbuf[pl.ds(SUBLANES - 2, tm), :])
    tail = gbuf[pl.ds(tm, SUBLANES), :]
    carry_scr[j] = tail
    cf_ref[...] = tail[SUBLANES - 2:, :]

    hid = (_gelu(gc) * val).astype(BF16)
    acc_scr[...] += _dot(hid, wd_ref[...])

    @pl.when(j == nj - 1)
    def _():
        o_ref[...] = _layer_norm(DN_ALPHA * x1_scr[...] + acc_scr[...], l2g_ref[...], l2b_ref[...])
        _sample_attn_finish(qs_ref, kns_ref, vns_ref, lfns_ref, gcol_ref, yas_ref,
                            m_scr, l_scr, sacc_scr, c_scr)


def _prompt_mix_ffn(layer, x, ya, yr, wo, l1g, l1b, wu, cw, cb, wd, l2g, l2b, seq,
                    page_table, q_col, k_col, v_col, lf_new, g_col, cache_kT, cache_vT, cache_lfT):
    M = x.shape[0]
    tm, fc = TM_FFN, FC
    nj = D_FF // fc
    tiles_per_seq = seq // tm
    n_s, n_pt = page_table.shape
    assert M // tm == n_s and n_pt % nj == 0, "one sample sequence per row tile, pages split over chunks"
    P = n_pt // nj
    kern = functools.partial(_prompt_mix_ffn_kernel, tm=tm, tiles_per_seq=tiles_per_seq, nj=nj,
                             n_pages=P)
    row = lambda i, j, pt: (i, 0)
    const = lambda i, j, pt: (0, 0)
    seq3 = lambda i, j, pt: (i, 0, 0)

    def page_spec(k, shape):
        nd = len(shape)
        return pl.BlockSpec((None, None) + shape,
                            lambda i, j, pt, k=k: (layer, pt[i, j * P + k]) + (0,) * nd)

    kv_shape = (N_HEADS, HEAD_DIM, PAGE_SIZE)
    in_specs = [
        pl.BlockSpec((tm, D_MODEL), row),
        pl.BlockSpec((tm, D_ATT), row),
        pl.BlockSpec((tm, D_RNN), row),
        pl.BlockSpec((D_MODEL, D_MODEL), const),
        pl.BlockSpec((1, D_MODEL), const),
        pl.BlockSpec((1, D_MODEL), const),
        pl.BlockSpec((D_MODEL, fc), lambda i, j, pt: (0, j)),
        pl.BlockSpec((D_MODEL, fc), lambda i, j, pt: (0, j + nj)),
        pl.BlockSpec((3, fc), lambda i, j, pt: (0, j)),
        pl.BlockSpec((1, fc), lambda i, j, pt: (0, j)),
        pl.BlockSpec((fc, D_MODEL), lambda i, j, pt: (j, 0)),
        pl.BlockSpec((1, D_MODEL), const),
        pl.BlockSpec((1, D_MODEL), const),
        pl.BlockSpec((None, D_ATT, 1), seq3),
        pl.BlockSpec((None, D_ATT, 1), seq3),
        pl.BlockSpec((None, D_ATT, 1), seq3),
        pl.BlockSpec((None, N_HEADS, 1), seq3),
        pl.BlockSpec((D_ATT, 1), const),
    ]
    in_specs += [page_spec(k, kv_shape) for k in range(P)]
    in_specs += [page_spec(k, kv_shape) for k in range(P)]
    in_specs += [page_spec(k, (N_HEADS, PAGE_SIZE)) for k in range(P)]
    grid_spec = pltpu.PrefetchScalarGridSpec(
        num_scalar_prefetch=1,
        grid=(M // tm, nj),
        in_specs=in_specs,
        out_specs=[
            pl.BlockSpec((tm, D_MODEL), row),
            pl.BlockSpec((None, 2, fc), lambda i, j, pt: (i, 0, j)),
            pl.BlockSpec((None, D_ATT, 1), seq3),
        ],
        scratch_shapes=[
            pltpu.VMEM((tm, D_MODEL), F32),
            pltpu.VMEM((tm, D_MODEL), BF16),
            pltpu.VMEM((tm, D_MODEL), F32),
            pltpu.VMEM((tm + SUBLANES, fc), F32),
            pltpu.VMEM((nj, SUBLANES, fc), F32),
            pltpu.VMEM((N_HEADS, HEAD_DIM, PAGE_SIZE), F32),
            pltpu.VMEM((N_HEADS, 1, 1), F32),
            pltpu.VMEM((N_HEADS, 1, PAGE_SIZE), F32),
            pltpu.VMEM((N_HEADS, HEAD_DIM, PAGE_SIZE), F32),
            pltpu.VMEM((N_HEADS, 1), F32),
        ],
    )
    return pl.pallas_call(
        kern,
        grid_spec=grid_spec,
        out_shape=[
            jax.ShapeDtypeStruct((M, D_MODEL), F32),
            jax.ShapeDtypeStruct((M // tm, 2, D_FF), F32),
            jax.ShapeDtypeStruct((n_s, D_ATT, 1), BF16),
        ],
        compiler_params=pltpu.CompilerParams(
            dimension_semantics=("arbitrary", "arbitrary"), vmem_limit_bytes=VMEM_LIMIT),
        name="prompt_mix_ffn",
    )(page_table, x, ya, yr, wo, l1g, l1b, wu, wu, cw, cb, wd, l2g, l2b,
      q_col, k_col, v_col, lf_new, g_col,
      *([cache_kT] * P), *([cache_vT] * P), *([cache_lfT] * P))


def _sample_mix_ffn_kernel(x_ref, ya_ref, yr_ref, wo_ref, l1g_ref, l1b_ref, wug_ref, wuv_ref, p0_ref,
                           p1_ref, cw_ref, cb_ref, wd_ref, l2g_ref, l2b_ref,
                           o_ref, gate_ref,
                           x1_scr, x1b_scr, acc_scr, *, nj):
    j = pl.program_id(0)

    @pl.when(j == 0)
    def _():
        _mix_head(x_ref, ya_ref, yr_ref, wo_ref, l1g_ref, l1b_ref, x1_scr, x1b_scr, acc_scr)

    x1b = x1b_scr[...]
    gate = _dot(x1b, wug_ref[...])
    val = _dot(x1b, wuv_ref[...])
    gate_ref[...] = gate
    gc = (cb_ref[...] + cw_ref[2:3, :] * gate + cw_ref[1:2, :] * p1_ref[...]
          + cw_ref[0:1, :] * p0_ref[...])
    hid = (_gelu(gc) * val).astype(BF16)
    acc_scr[...] += _dot(hid, wd_ref[...])

    @pl.when(j == nj - 1)
    def _():
        o_ref[...] = _layer_norm(DN_ALPHA * x1_scr[...] + acc_scr[...], l2g_ref[...], l2b_ref[...])


def _sample_mix_ffn(x, ya, yr, wo, l1g, l1b, wu, p0, p1, cw, cb, wd, l2g, l2b):
    n = x.shape[0]
    fc = FC
    nj = D_FF // fc
    kern = functools.partial(_sample_mix_ffn_kernel, nj=nj)
    const = lambda j: (0, 0)
    chunk = lambda j: (0, j)
    return pl.pallas_call(
        kern,
        grid=(nj,),
        in_specs=[
            pl.BlockSpec((n, D_MODEL), const),
            pl.BlockSpec((n, D_ATT), const),
            pl.BlockSpec((n, D_RNN), const),
            pl.BlockSpec((D_MODEL, D_MODEL), const),
            pl.BlockSpec((1, D_MODEL), const),
            pl.BlockSpec((1, D_MODEL), const),
            pl.BlockSpec((D_MODEL, fc), chunk),
            pl.BlockSpec((D_MODEL, fc), lambda j: (0, j + nj)),
            pl.BlockSpec((n, fc), chunk),
            pl.BlockSpec((n, fc), chunk),
            pl.BlockSpec((3, fc), chunk),
            pl.BlockSpec((1, fc), chunk),
            pl.BlockSpec((fc, D_MODEL), lambda j: (j, 0)),
            pl.BlockSpec((1, D_MODEL), const),
            pl.BlockSpec((1, D_MODEL), const),
        ],
        out_specs=[
            pl.BlockSpec((n, D_MODEL), const),
            pl.BlockSpec((n, fc), chunk),
        ],
        out_shape=[
            jax.ShapeDtypeStruct((n, D_MODEL), F32),
            jax.ShapeDtypeStruct((n, D_FF), F32),
        ],
        scratch_shapes=[
            pltpu.VMEM((n, D_MODEL), F32),
            pltpu.VMEM((n, D_MODEL), BF16),
            pltpu.VMEM((n, D_MODEL), F32),
        ],
        compiler_params=pltpu.CompilerParams(
            dimension_semantics=("arbitrary",), vmem_limit_bytes=VMEM_LIMIT),
        name="sample_mix_ffn",
    )(x, ya, yr, wo, l1g, l1b, wu, wu, p0, p1, cw, cb, wd, l2g, l2b)


def _sample_inproj_kernel(x_ref, wn_ref, wt_ref, bf_ref, cs_ref, h0_ref, cw_ref, cb_ref, wg_ref, bg_ref,
                          lam_ref, grnn_ref,
                          q_ref, k_ref, v_ref, lf_ref, yr_ref, h_ref, csn_ref):
    xb = x_ref[...].astype(BF16)
    yn = _dot(xb, wn_ref[...])
    yt = _dot_nt(xb, wt_ref[...])
    q_ref[...] = yt[:, 0:D_ATT]
    k_ref[...] = yt[:, D_ATT:2 * D_ATT]
    v_ref[...] = yt[:, 2 * D_ATT:3 * D_ATT]
    lf_ref[...] = _log_sigmoid(yt[:, 3 * D_ATT:3 * D_ATT + N_HEADS] + bf_ref[...])

    xr = yn[:, D_ATT + LANES:D_ATT + LANES + D_RNN]
    gr = yn[:, D_ATT + LANES + D_RNN:]
    xc = (cb_ref[...] + cw_ref[3:4, :] * xr + cw_ref[2:3, :] * cs_ref[2]
          + cw_ref[1:2, :] * cs_ref[1] + cw_ref[0:1, :] * cs_ref[0])
    csn_ref[0] = cs_ref[1]
    csn_ref[1] = cs_ref[2]
    csn_ref[2] = xr

    a, u = _rglru_gates(xc, wg_ref, bg_ref, lam_ref)
    h = a * h0_ref[...] + u
    h_ref[...] = h
    yr_ref[...] = _rms_norm(h * _gelu(gr), grnn_ref[...]).astype(BF16)


def _sample_inproj(x, wn, wt, bf_row, cs, h0, cw, cb, wg, bg, lam, grnn):
    n = x.shape[0]
    return pl.pallas_call(
        _sample_inproj_kernel,
        out_shape=[
            jax.ShapeDtypeStruct((n, D_ATT), F32),
            jax.ShapeDtypeStruct((n, D_ATT), F32),
            jax.ShapeDtypeStruct((n, D_ATT), F32),
            jax.ShapeDtypeStruct((n, N_HEADS), F32),
            jax.ShapeDtypeStruct((n, D_RNN), BF16),
            jax.ShapeDtypeStruct((n, D_RNN), F32),
            jax.ShapeDtypeStruct((3, n, D_RNN), F32),
        ],
        compiler_params=pltpu.CompilerParams(vmem_limit_bytes=VMEM_LIMIT),
        name="sample_inproj",
    )(x, wn, wt, bf_row, cs, h0, cw, cb, wg, bg, lam, grnn)


def _sample_attn_init(q_ref, qb_scr, m_scr, l_scr, acc_scr, c_scr):
    qb_scr[...] = jnp.broadcast_to(q_ref[...].reshape(N_HEADS, HEAD_DIM, 1) * ATT_SCALE,
                                   (N_HEADS, HEAD_DIM, PAGE_SIZE))
    m_scr[...] = jnp.full(m_scr.shape, NEG_BIG, F32)
    l_scr[...] = jnp.zeros(l_scr.shape, F32)
    acc_scr[...] = jnp.zeros(acc_scr.shape, F32)
    c_scr[...] = jnp.zeros(c_scr.shape, F32)


def _sample_attn_pages(k_refs, v_refs, lf_refs, qb_scr, m_scr, l_scr, acc_scr, c_scr):
    n_pages = len(k_refs)
    cs, run = _running_sum_blocks([lf_refs[i][...] for i in range(n_pages)], c_scr[...])
    c_scr[...] = run
    qb = qb_scr[...]
    scores = []
    for i in range(n_pages):
        s_i = jnp.sum(k_refs[i][...] * qb, axis=1, keepdims=True)
        scores.append(s_i - cs[i].reshape(N_HEADS, 1, PAGE_SIZE))
    s = jnp.concatenate(scores, axis=-1)

    m = m_scr[...]
    m_new = jnp.maximum(m, jnp.max(s, axis=-1, keepdims=True))
    alpha = jnp.exp(m - m_new)
    p = jnp.exp(s - m_new)
    l_new = alpha * l_scr[...]
    acc = alpha * acc_scr[...]
    for i in range(n_pages):
        p_i = p[:, :, i * PAGE_SIZE:(i + 1) * PAGE_SIZE]
        l_new = l_new + p_i
        acc = acc + v_refs[i][...] * p_i
    l_scr[...] = l_new
    acc_scr[...] = acc
    m_scr[...] = m_new


def _sample_attn_finish(q_ref, kn_ref, vn_ref, lfn_ref, g_ref, o_ref, m_scr, l_scr, acc_scr, c_scr):
    hds = (N_HEADS, HEAD_DIM, 1)
    c_new = (c_scr[...] + lfn_ref[...]).reshape(N_HEADS, 1, 1)
    qc = q_ref[...].reshape(hds) * ATT_SCALE
    s_new = jnp.sum(qc * kn_ref[...].reshape(hds), axis=1, keepdims=True) - c_new
    m_old = m_scr[...]
    m_fin = jnp.maximum(m_old, s_new)
    a2 = jnp.exp(m_old - m_fin)
    p_new = jnp.exp(s_new - m_fin)
    l_fin = a2 * jnp.sum(l_scr[...], axis=-1, keepdims=True) + p_new
    o3 = (a2 * jnp.sum(acc_scr[...], axis=-1, keepdims=True)
          + p_new * vn_ref[...].reshape(hds)) / l_fin
    y = o3.reshape(D_ATT, 1)
    ms = jnp.mean(y * y, axis=0, keepdims=True)
    o_ref[...] = (y * lax.rsqrt(ms + EPS) * g_ref[...]).astype(BF16)


def _block_diag(w):
    n, c, d = w.shape
    return jnp.einsum("ncd,nm->ncmd", w, jnp.eye(n, dtype=w.dtype)).reshape(n * c, n * d)


def kernel(x_prompt, x_sample, cache_k, cache_v, cache_logf, page_table, state_h, state_conv_rnn,
           state_conv_ffn, w_in, b_f, rnn_conv_w, rnn_conv_b, w_a, b_a, w_x, b_x, lam, g_att, g_rnn,
           w_out, ln1_g, ln1_b, w_up, ffn_conv_w, ffn_conv_b, w_down, ln2_g, ln2_b):
    B, T, _ = x_prompt.shape
    n_s = x_sample.shape[0]

    cache_kT = jnp.transpose(cache_k, (0, 1, 3, 4, 2))
    cache_vT = jnp.transpose(cache_v, (0, 1, 3, 4, 2))
    cache_lfT = jnp.transpose(cache_logf, (0, 1, 3, 2))

    x_p = x_prompt
    x_s = x_sample.reshape(n_s, D_MODEL)
    outs = [[] for _ in range(12)]
    stacked = None
    c1, c2, c3 = D_ATT, 2 * D_ATT, 3 * D_ATT
    c4 = c3 + N_HEADS
    c5 = c4 + D_RNN
    for l in range(DEPTH):
        wi = w_in[l]
        wn = jnp.concatenate([wi[:, c1:c2], wi[:, c3:c4], jnp.zeros((D_MODEL, LANES - N_HEADS), F32),
                              wi[:, c4:c5], wi[:, c5:]], axis=1).astype(BF16)
        wt = jnp.concatenate([wi[:, :c4], jnp.zeros((D_MODEL, 8), F32)], axis=1).T.astype(BF16)
        bf_row = jnp.concatenate([b_f[l], jnp.zeros((LANES - N_HEADS,), F32)])[None, :]
        wg = jnp.concatenate([_block_diag(w_a[l]), _block_diag(w_x[l])], axis=1).astype(BF16)
        bg = jnp.concatenate([b_a[l], b_x[l]])[None, :]
        wo = w_out[l].astype(BF16)
        wu = w_up[l].astype(BF16)
        wd = w_down[l].astype(BF16)
        row = lambda v: v[None, :]
        cw, cb = rnn_conv_w[l], row(rnn_conv_b[l])
        lam_l, grnn, g_col = row(lam[l]), row(g_rnn[l]), g_att[l][:, None]
        l1g, l1b, l2g, l2b = row(ln1_g[l]), row(ln1_b[l]), row(ln2_g[l]), row(ln2_b[l])
        fcw, fcb = ffn_conv_w[l], row(ffn_conv_b[l])

        cs = jnp.transpose(state_conv_rnn[l], (1, 0, 2))
        qs, ks, vs, lfs, yrs, hs, csn = _sample_inproj(
            x_s, wn, wt, b_f[l][None, :], cs, state_h[l], cw, cb, wg, bg, lam_l, grnn)

        qblk, kT_all, vT_all, kblk, vblk, lfT_all, yr, h_last, cr = _prompt_inproj(
            l, x_p, wn, wt, b_f[l][:, None], bf_row, cw, cb, wg, bg, lam_l, grnn, stacked)
        stacked = (kT_all, vT_all, lfT_all)
        ya = _prompt_attn(qblk, kblk, vblk, g_col)
        x2, cf, yas = _prompt_mix_ffn(
            l, x_p.reshape(B * T, D_MODEL), ya.reshape(B * T, D_ATT), yr.reshape(B * T, D_RNN),
            wo, l1g, l1b, wu, fcw, fcb, wd, l2g, l2b, T,
            page_table, qs.reshape(n_s, D_ATT, 1), ks.reshape(n_s, D_ATT, 1),
            vs.reshape(n_s, D_ATT, 1), lfs.reshape(n_s, N_HEADS, 1), g_col,
            cache_kT, cache_vT, cache_lfT)
        x_p = x2.reshape(B, T, D_MODEL)
        outs[3].append(h_last.reshape(B, D_RNN))
        outs[4].append(cr)
        tiles_per_seq = T // TM_FFN
        outs[5].append(cf[tiles_per_seq - 1::tiles_per_seq])

        p0 = state_conv_ffn[l][:, 0, :]
        p1 = state_conv_ffn[l][:, 1, :]
        x_s, gate_s = _sample_mix_ffn(x_s, yas.reshape(n_s, D_ATT), yrs, wo, l1g, l1b, wu, p0, p1,
                                      fcw, fcb, wd, l2g, l2b)
        outs[6].append(ks.reshape(n_s, 1, N_HEADS, HEAD_DIM))
        outs[7].append(vs.reshape(n_s, 1, N_HEADS, HEAD_DIM))
        outs[8].append(lfs.reshape(n_s, 1, N_HEADS))
        outs[9].append(hs)
        outs[10].append(jnp.transpose(csn, (1, 0, 2)))
        outs[11].append(jnp.stack([p1, gate_s], axis=1))

    st = [jnp.stack(o) for o in outs[3:]]
    kT_all, vT_all, lfT_all = stacked
    k_p = kT_all.reshape(DEPTH, B, N_HEADS, HEAD_DIM, T).transpose(0, 1, 4, 2, 3)
    v_p = vT_all.reshape(DEPTH, B, N_HEADS, HEAD_DIM, T).transpose(0, 1, 4, 2, 3)
    lf_p = lfT_all.transpose(0, 1, 3, 2)
    return (x_p, x_s.reshape(n_s, 1, D_MODEL), k_p, v_p, lf_p, *st)
```

```python
import functools

import jax
import jax.numpy as jnp
from jax import lax
from jax.experimental import pallas as pl
from jax.experimental.pallas import tpu as pltpu

F32 = jnp.float32
BF16 = jnp.bfloat16

D_MODEL = 1024
D_ATT = 512
D_RNN = 512
N_HEADS = 8
HEAD_DIM = 64
N_RNN_BLOCKS = 8
D_FF = 3072
RGLRU_C = 8.0
DEPTH = 2
PAGE_SIZE = 128
DN_ALPHA = (2 * DEPTH) ** 0.25
EPS = 1e-5
ATT_SCALE = HEAD_DIM ** -0.5
NEG_BIG = -1e30

LANES = 128
SUBLANES = 8
VMEM_LIMIT = 56 * 1024 * 1024

TM_PROJ = 512
TQ = 256
TK = 128
TM_FFN = 512
FC = 768


def _softplus(x):
    return jnp.maximum(x, 0.0) + jnp.log1p(jnp.exp(-jnp.abs(x)))


def _log_sigmoid(x):
    return -_softplus(-x)


def _gelu(x):
    c = (2.0 / jnp.pi) ** 0.5
    return 0.5 * x * (1.0 + jnp.tanh(c * (x + 0.044715 * (x * x * x))))


def _layer_norm(x, g, b):
    mu = jnp.mean(x, axis=-1, keepdims=True)
    xc = x - mu
    var = jnp.mean(xc * xc, axis=-1, keepdims=True)
    return xc * lax.rsqrt(var + EPS) * g + b


def _rms_norm(x, g):
    return x * lax.rsqrt(jnp.mean(x * x, axis=-1, keepdims=True) + EPS) * g


def _dot(a, b):
    return jnp.dot(a, b, preferred_element_type=F32)


def _dot_nt(a, b):
    return lax.dot_general(a, b, (((1,), (1,)), ((), ())), preferred_element_type=F32)


def _prefix_sum_lanes(x, tri):
    hi = x.astype(BF16)
    r1 = x - hi.astype(F32)
    mid = r1.astype(BF16)
    lo = (r1 - mid.astype(F32)).astype(BF16)
    return _dot(hi, tri) + _dot(mid, tri) + _dot(lo, tri)


def _tri128():
    r = lax.broadcasted_iota(jnp.int32, (LANES, LANES), 0)
    c = lax.broadcasted_iota(jnp.int32, (LANES, LANES), 1)
    return (r <= c).astype(BF16)


def _running_sum_blocks(blocks, run):
    tri = _tri128()
    pre = [_prefix_sum_lanes(b, tri) for b in blocks]
    out = []
    for p in pre:
        out.append(p + run)
        run = run + p[:, LANES - 1:LANES]
    return out, run


def _split3(x):
    hi = x.astype(BF16).astype(F32)
    r1 = x - hi
    mid = r1.astype(BF16).astype(F32)
    lo = (r1 - mid).astype(BF16).astype(F32)
    return hi, mid, lo


def _rglru_gates(xc, wg_ref, bg_ref, lam_ref):
    g = _dot(xc.astype(BF16), wg_ref[...]) + bg_ref[...]
    r = jax.nn.sigmoid(g[:, :D_RNN])
    i = jax.nn.sigmoid(g[:, D_RNN:])
    log_a = (-RGLRU_C * _softplus(-lam_ref[...])) * r
    a = jnp.exp(log_a)
    mult = jnp.sqrt(-jnp.tanh(log_a) * (a * a + 1.0))
    return a, mult * (i * xc)


def _prompt_inproj_kernel(*refs, tm, tq, tk, n_alias):
    (x_ref, wn_ref, wt_ref, bfc_ref, bfr_ref, cw_ref, cb_ref, wg_ref, bg_ref, lam_ref,
     grnn_ref) = refs[:11]
    (qblk_ref, kT_ref, vT_ref, kblk_ref, vblk_ref, lfT_ref, yr_ref, hl_ref, cr_ref,
     xbuf, a_scr, u_scr, h_scr, hc_scr, crun_scr) = refs[11 + n_alias:]
    t = pl.program_id(1)

    @pl.when(t == 0)
    def _():
        xbuf[pl.ds(0, SUBLANES), :] = jnp.zeros((SUBLANES, D_RNN), F32)
        hc_scr[...] = jnp.zeros((SUBLANES, D_RNN), F32)
        crun_scr[...] = jnp.zeros(crun_scr.shape, F32)

    xb = x_ref[...].astype(BF16)
    yn = _dot(xb, wn_ref[...])
    yt = _dot_nt(wt_ref[...], xb)

    qT = yt[0:D_ATT] * ATT_SCALE
    kT = yt[D_ATT:2 * D_ATT]
    vT = yt[2 * D_ATT:3 * D_ATT]
    kT_ref[...] = kT
    vT_ref[...] = vT
    lfT_ref[...] = _log_sigmoid(yt[3 * D_ATT:3 * D_ATT + N_HEADS] + bfc_ref[...])

    lf_rows = _log_sigmoid(yn[:, D_ATT:D_ATT + LANES] + bfr_ref[...])
    ri = lax.broadcasted_iota(jnp.int32, (LANES, LANES), 0)
    ci = lax.broadcasted_iota(jnp.int32, (LANES, LANES), 1)
    lower = (ci <= ri).astype(BF16)
    run = crun_scr[0:1, :]
    c_rows = []
    for j in range(tm // LANES):
        hi, mid, lo = _split3(lf_rows[j * LANES:(j + 1) * LANES])
        pre = (_dot(lower, hi.astype(BF16)) + _dot(lower, mid.astype(BF16))
               + _dot(lower, lo.astype(BF16)))
        c_rows.append(pre + run)
        run = run + pre[LANES - 1:LANES, :]
    crun_scr[...] = jnp.broadcast_to(run, crun_scr.shape)
    nhi, nmid, nlo = _split3(-jnp.concatenate(c_rows, axis=0))

    lane = lax.broadcasted_iota(jnp.int32, (tm, LANES), 1)
    is_head = lane < N_HEADS
    packed = jnp.where(is_head, nhi, 0.0)
    packed = packed + pltpu.roll(jnp.where(is_head, nmid, 0.0), N_HEADS, 1)
    packed = packed + pltpu.roll(jnp.where(is_head, nlo, 0.0), 2 * N_HEADS, 1)
    src = lax.broadcasted_iota(jnp.int32, (LANES, N_HEADS * LANES), 0)
    dst = lax.broadcasted_iota(jnp.int32, (LANES, N_HEADS * LANES), 1)
    src_head = src % N_HEADS
    src_term = src // N_HEADS
    base = jnp.where(src_head % 2 == 0, HEAD_DIM, 0)
    place = ((dst // LANES == src_head) & (dst % LANES == base + src_term)
             & (src_term < 3)).astype(BF16)
    bias_lanes = _dot(packed.astype(BF16), place)

    ones = jnp.ones((HEAD_DIM, tk), BF16)
    rowi = lax.broadcasted_iota(jnp.int32, (HEAD_DIM, tq), 0)
    sel_rows = jnp.where(rowi < 3, 1.0, 0.0).astype(BF16)
    for h in range(N_HEADS):
        pair = yn[:, (h // 2) * LANES:(h // 2 + 1) * LANES]
        own = (lane < HEAD_DIM) if h % 2 == 0 else (lane >= HEAD_DIM)
        kb = (jnp.where(own, pair, 0.0) + bias_lanes[:, h * LANES:(h + 1) * LANES]).astype(BF16)
        chan = slice(0, HEAD_DIM) if h % 2 == 0 else slice(HEAD_DIM, 2 * HEAD_DIM)
        rest = slice(HEAD_DIM, 2 * HEAD_DIM) if h % 2 == 0 else slice(0, HEAD_DIM)
        hrows = slice(h * HEAD_DIM, (h + 1) * HEAD_DIM)
        for c in range(tm // tk):
            cols = slice(c * tk, (c + 1) * tk)
            kblk_ref[c, h] = kb[cols, :]
            vblk_ref[c, h, chan, :] = vT[hrows, cols].astype(BF16)
            vblk_ref[c, h, rest, :] = ones
        for c in range(tm // tq):
            cols = slice(c * tq, (c + 1) * tq)
            qblk_ref[c, h, chan, :] = qT[hrows, cols].astype(BF16)
            qblk_ref[c, h, rest, :] = sel_rows

    xr = yn[:, D_ATT + LANES:D_ATT + LANES + D_RNN]
    gr = yn[:, D_ATT + LANES + D_RNN:]
    xbuf[pl.ds(SUBLANES, tm), :] = xr
    xc = (cb_ref[...] + cw_ref[3:4, :] * xr
          + cw_ref[2:3, :] * xbuf[pl.ds(SUBLANES - 1, tm), :]
          + cw_ref[1:2, :] * xbuf[pl.ds(SUBLANES - 2, tm), :]
          + cw_ref[0:1, :] * xbuf[pl.ds(SUBLANES - 3, tm), :])
    tail = xbuf[pl.ds(tm, SUBLANES), :]
    xbuf[pl.ds(0, SUBLANES), :] = tail
    cr_ref[...] = tail[SUBLANES - 3:, :]

    a, u = _rglru_gates(xc, wg_ref, bg_ref, lam_ref)
    a_scr[...] = a
    u_scr[...] = u

    sub = lax.broadcasted_iota(jnp.int32, (SUBLANES, D_RNN), 0)

    def group(g, hin):
        r0 = g * SUBLANES
        ag = a_scr[pl.ds(r0, SUBLANES), :]
        ug = u_scr[pl.ds(r0, SUBLANES), :]
        for d in (1, 2, 4):
            keep = sub >= d
            a_sh = pltpu.roll(ag, d, 0)
            u_sh = pltpu.roll(ug, d, 0)
            ug = jnp.where(keep, ag * u_sh + ug, ug)
            ag = jnp.where(keep, ag * a_sh, ag)
        hg = ag * hin + ug
        h_scr[pl.ds(r0, SUBLANES), :] = hg
        return jnp.broadcast_to(hg[SUBLANES - 1:SUBLANES, :], (SUBLANES, D_RNN))

    hin = hc_scr[...]
    for g in range(tm // SUBLANES):
        hin = group(g, hin)
    hc_scr[...] = hin
    hl_ref[...] = hin[0:1, :]

    y = h_scr[...] * _gelu(gr)
    yr_ref[...] = _rms_norm(y, grnn_ref[...]).astype(BF16)


def _prompt_inproj(layer, x, wn, wt, bf_col, bf_row, cw, cb, wg, bg, lam, grnn, stacked):
    B, T, _ = x.shape
    tm, tq, tk = TM_PROJ, TQ, TK
    nt = T // tm
    wt_rows = wt.shape[0]
    wn_cols = wn.shape[1]
    const = lambda b, t: (0, 0)
    n_alias = 0 if stacked is None else 3
    kern = functools.partial(_prompt_inproj_kernel, tm=tm, tq=tq, tk=tk, n_alias=n_alias)
    in_specs = [
        pl.BlockSpec((None, tm, D_MODEL), lambda b, t: (b, t, 0)),
        pl.BlockSpec((D_MODEL, wn_cols), const),
        pl.BlockSpec((wt_rows, D_MODEL), const),
        pl.BlockSpec((N_HEADS, 1), const),
        pl.BlockSpec((1, LANES), const),
        pl.BlockSpec((4, D_RNN), const),
        pl.BlockSpec((1, D_RNN), const),
        pl.BlockSpec((D_RNN, 2 * D_RNN), const),
        pl.BlockSpec((1, 2 * D_RNN), const),
        pl.BlockSpec((1, D_RNN), const),
        pl.BlockSpec((1, D_RNN), const),
    ] + [pl.BlockSpec(memory_space=pl.ANY)] * n_alias
    args = (x, wn, wt, bf_col, bf_row, cw, cb, wg, bg, lam, grnn)
    args += () if stacked is None else tuple(stacked)
    q_blk = (None, tm // tq, N_HEADS, 2 * HEAD_DIM, tq)
    v_blk = (None, tm // tk, N_HEADS, 2 * HEAD_DIM, tk)
    k_blk = (None, tm // tk, N_HEADS, tk, 2 * HEAD_DIM)
    blk_map = lambda b, t: (b, t, 0, 0, 0)
    return pl.pallas_call(
        kern,
        grid=(B, nt),
        in_specs=in_specs,
        out_specs=[
            pl.BlockSpec(q_blk, blk_map),
            pl.BlockSpec((None, None, D_ATT, tm), lambda b, t: (layer, b, 0, t)),
            pl.BlockSpec((None, None, D_ATT, tm), lambda b, t: (layer, b, 0, t)),
            pl.BlockSpec(k_blk, blk_map),
            pl.BlockSpec(v_blk, blk_map),
            pl.BlockSpec((None, None, N_HEADS, tm), lambda b, t: (layer, b, 0, t)),
            pl.BlockSpec((None, tm, D_RNN), lambda b, t: (b, t, 0)),
            pl.BlockSpec((None, 1, D_RNN), lambda b, t: (b, 0, 0)),
            pl.BlockSpec((None, 3, D_RNN), lambda b, t: (b, 0, 0)),
        ],
        out_shape=[
            jax.ShapeDtypeStruct((B, T // tq, N_HEADS, 2 * HEAD_DIM, tq), BF16),
            jax.ShapeDtypeStruct((DEPTH, B, D_ATT, T), F32),
            jax.ShapeDtypeStruct((DEPTH, B, D_ATT, T), F32),
            jax.ShapeDtypeStruct((B, T // tk, N_HEADS, tk, 2 * HEAD_DIM), BF16),
            jax.ShapeDtypeStruct((B, T // tk, N_HEADS, 2 * HEAD_DIM, tk), BF16),
            jax.ShapeDtypeStruct((DEPTH, B, N_HEADS, T), F32),
            jax.ShapeDtypeStruct((B, T, D_RNN), BF16),
            jax.ShapeDtypeStruct((B, 1, D_RNN), F32),
            jax.ShapeDtypeStruct((B, 3, D_RNN), F32),
        ],
        scratch_shapes=[
            pltpu.VMEM((tm + SUBLANES, D_RNN), F32),
            pltpu.VMEM((tm, D_RNN), F32),
            pltpu.VMEM((tm, D_RNN), F32),
            pltpu.VMEM((tm, D_RNN), F32),
            pltpu.VMEM((SUBLANES, D_RNN), F32),
            pltpu.VMEM((N_HEADS, LANES), F32),
        ],
        input_output_aliases={} if stacked is None else {11: 1, 12: 2, 13: 5},
        compiler_params=pltpu.CompilerParams(
            dimension_semantics=("arbitrary", "arbitrary"), vmem_limit_bytes=VMEM_LIMIT),
        name="prompt_inproj",
    )(*args)


def _prompt_attn_kernel(q_ref, k_ref, v_ref, g_ref, o_ref, m_scr, acc_scr, *, tq, tk):
    qi = pl.program_id(1)
    kpq = tq // tk
    key = lax.broadcasted_iota(jnp.int32, (tk, tq), 0)
    qry = lax.broadcasted_iota(jnp.int32, (tk, tq), 1)

    for h in range(N_HEADS):
        m_scr[h] = jnp.full((1, tq), NEG_BIG, F32)
        acc_scr[h] = jnp.zeros((LANES, tq), F32)

    def tile(kj, diag):
        for h in range(N_HEADS):
            s = _dot(k_ref[kj, h], q_ref[h])
            if diag is not None:
                s = jnp.where(key + diag * tk <= qry, s, NEG_BIG)
            m = m_scr[h]
            m_new = jnp.maximum(m, jnp.max(s, axis=0, keepdims=True))
            alpha = jnp.exp(m - m_new)
            p = jnp.exp(s - m_new).astype(BF16)
            acc_scr[h] = alpha * acc_scr[h] + _dot(v_ref[kj, h], p)
            m_scr[h] = m_new

    def full_tiles(j, carry):
        for d in range(kpq):
            tile(j * kpq + d, None)
        return carry

    lax.fori_loop(0, qi, full_tiles, 0)
    for d in range(kpq):
        tile(qi * kpq + d, d)

    halves = []
    for h in range(N_HEADS):
        acc = acc_scr[h]
        if h % 2 == 0:
            halves.append(acc[:HEAD_DIM] / acc[HEAD_DIM:])
        else:
            halves.append(acc[HEAD_DIM:] / acc[:HEAD_DIM])
    yT = jnp.concatenate(halves, axis=0)
    ms = jnp.mean(yT * yT, axis=0, keepdims=True)
    ynT = (yT * lax.rsqrt(ms + EPS) * g_ref[...]).astype(BF16)
    er = lax.broadcasted_iota(jnp.int32, (tq, tq), 0)
    ec = lax.broadcasted_iota(jnp.int32, (tq, tq), 1)
    o_ref[...] = _dot_nt((er == ec).astype(BF16), ynT).astype(BF16)


def _prompt_attn(qblk, kblk, vblk, g_col):
    B, nq, _, _, tq = qblk.shape
    nk, tk = kblk.shape[1], kblk.shape[3]
    T = nq * tq
    kern = functools.partial(_prompt_attn_kernel, tq=tq, tk=tk)
    return pl.pallas_call(
        kern,
        grid=(B, nq),
        in_specs=[
            pl.BlockSpec((None, None, N_HEADS, 2 * HEAD_DIM, tq), lambda b, i: (b, i, 0, 0, 0)),
            pl.BlockSpec((None, nk, N_HEADS, tk, 2 * HEAD_DIM), lambda b, i: (b, 0, 0, 0, 0)),
            pl.BlockSpec((None, nk, N_HEADS, 2 * HEAD_DIM, tk), lambda b, i: (b, 0, 0, 0, 0)),
            pl.BlockSpec((D_ATT, 1), lambda b, i: (0, 0)),
        ],
        out_specs=pl.BlockSpec((None, tq, D_ATT), lambda b, i: (b, i, 0)),
        out_shape=jax.ShapeDtypeStruct((B, T, D_ATT), BF16),
        scratch_shapes=[
            pltpu.VMEM((N_HEADS, 1, tq), F32),
            pltpu.VMEM((N_HEADS, LANES, tq), F32),
        ],
        compiler_params=pltpu.CompilerParams(
            dimension_semantics=("arbitrary", "arbitrary"), vmem_limit_bytes=VMEM_LIMIT),
        name="prompt_attn",
    )(qblk, kblk, vblk, g_col)


def _mix_head(x_ref, ya_ref, yr_ref, wo_ref, l1g_ref, l1b_ref, x1_scr, x1b_scr, acc_scr):
    mix = _dot(ya_ref[...], wo_ref[0:D_ATT, :]) + _dot(yr_ref[...], wo_ref[D_ATT:, :])
    x1 = _layer_norm(DN_ALPHA * x_ref[...] + mix, l1g_ref[...], l1b_ref[...])
    x1_scr[...] = x1
    x1b_scr[...] = x1.astype(BF16)
    acc_scr[...] = jnp.zeros(acc_scr.shape, F32)


def _prompt_mix_ffn_kernel(pt_ref, x_ref, ya_ref, yr_ref, wo_ref, l1g_ref, l1b_ref, wug_ref, wuv_ref,
                           cw_ref, cb_ref, wd_ref, l2g_ref, l2b_ref,
                           qkvT_ref, lfns_ref, gcol_ref, *rest,
                           tm, tiles_per_seq, nj, n_pages):
    k_refs = rest[:n_pages]
    v_refs = rest[n_pages:2 * n_pages]
    lf_refs = rest[2 * n_pages:3 * n_pages]
    (o_ref, cf_ref, yas_ref, x1_scr, x1b_scr, acc_scr, gbuf, carry_scr,
     qb_scr, m_scr, l_scr, sacc_scr, c_scr) = rest[3 * n_pages:]
    del pt_ref
    i = pl.program_id(0)
    j = pl.program_id(1)

    @pl.when(j == 0)
    def _():
        _mix_head(x_ref, ya_ref, yr_ref, wo_ref, l1g_ref, l1b_ref, x1_scr, x1b_scr, acc_scr)
        _sample_attn_init(i, qkvT_ref, qb_scr, m_scr, l_scr, sacc_scr, c_scr)

    _sample_attn_pages(k_refs, v_refs, lf_refs, qb_scr, m_scr, l_scr, sacc_scr, c_scr)

    x1b = x1b_scr[...]
    gate = _dot(x1b, wug_ref[...])
    val = _dot(x1b, wuv_ref[...])

    first = (i % tiles_per_seq) == 0
    prev = jnp.where(first, 0.0, carry_scr[j])
    gbuf[pl.ds(0, SUBLANES), :] = prev
    gbuf[pl.ds(SUBLANES, tm), :] = gate
    gc = (cb_ref[...] + cw_ref[2:3, :] * gate
          + cw_ref[1:2, :] * gbuf[pl.ds(SUBLANES - 1, tm), :]
          + cw_ref[0:1, :] * gbuf[pl.ds(SUBLANES - 2, tm), :])
    tail = gbuf[pl.ds(tm, SUBLANES), :]
    carry_scr[j] = tail
    cf_ref[...] = tail[SUBLANES - 2:, :]

    hid = (_gelu(gc) * val).astype(BF16)
    acc_scr[...] += _dot(hid, wd_ref[j])

    @pl.when(j == nj - 1)
    def _():
        o_ref[...] = _layer_norm(DN_ALPHA * x1_scr[...] + acc_scr[...], l2g_ref[...], l2b_ref[...])
        _sample_attn_finish(i, qkvT_ref, lfns_ref, gcol_ref, yas_ref,
                            qb_scr, m_scr, l_scr, sacc_scr, c_scr)


def _prompt_mix_ffn(layer, x, ya, yr, wo, l1g, l1b, wu, cw, cb, wd, l2g, l2b, seq,
                    page_table, qkvT, lf_new, g_col, cache_kT, cache_vT, cache_lfT):
    M = x.shape[0]
    tm, fc = TM_FFN, FC
    nj = D_FF // fc
    tiles_per_seq = seq // tm
    n_s, n_pt = page_table.shape
    assert M // tm == n_s and n_pt % nj == 0, "one sample sequence per row tile, pages split over chunks"
    P = n_pt // nj
    kern = functools.partial(_prompt_mix_ffn_kernel, tm=tm, tiles_per_seq=tiles_per_seq, nj=nj,
                             n_pages=P)
    row = lambda i, j, pt: (i, 0)
    const = lambda i, j, pt: (0, 0)
    seq3 = lambda i, j, pt: (i, 0, 0)

    def page_spec(k, shape):
        nd = len(shape)
        return pl.BlockSpec((None, None) + shape,
                            lambda i, j, pt, k=k: (layer, pt[i, j * P + k]) + (0,) * nd)

    kv_shape = (N_HEADS, HEAD_DIM, PAGE_SIZE)
    in_specs = [
        pl.BlockSpec((tm, D_MODEL), row),
        pl.BlockSpec((tm, D_ATT), row),
        pl.BlockSpec((tm, D_RNN), row),
        pl.BlockSpec((D_MODEL, D_MODEL), const, pipeline_mode=pl.Buffered(1)),
        pl.BlockSpec((1, D_MODEL), const),
        pl.BlockSpec((1, D_MODEL), const),
        pl.BlockSpec((D_MODEL, fc), lambda i, j, pt: (0, j)),
        pl.BlockSpec((D_MODEL, fc), lambda i, j, pt: (0, j + nj)),
        pl.BlockSpec((3, fc), lambda i, j, pt: (0, j)),
        pl.BlockSpec((1, fc), lambda i, j, pt: (0, j)),
        pl.BlockSpec((nj, fc, D_MODEL), lambda i, j, pt: (0, 0, 0), pipeline_mode=pl.Buffered(1)),
        pl.BlockSpec((1, D_MODEL), const),
        pl.BlockSpec((1, D_MODEL), const),
        pl.BlockSpec((3 * D_ATT, n_s), const),
        pl.BlockSpec((None, N_HEADS, 1), seq3),
        pl.BlockSpec((D_ATT, 1), const),
    ]
    in_specs += [page_spec(k, kv_shape) for k in range(P)]
    in_specs += [page_spec(k, kv_shape) for k in range(P)]
    in_specs += [page_spec(k, (N_HEADS, PAGE_SIZE)) for k in range(P)]
    grid_spec = pltpu.PrefetchScalarGridSpec(
        num_scalar_prefetch=1,
        grid=(M // tm, nj),
        in_specs=in_specs,
        out_specs=[
            pl.BlockSpec((tm, D_MODEL), row),
            pl.BlockSpec((None, 2, fc), lambda i, j, pt: (i, 0, j)),
            pl.BlockSpec((None, 1, D_ATT), seq3),
        ],
        scratch_shapes=[
            pltpu.VMEM((tm, D_MODEL), F32),
            pltpu.VMEM((tm, D_MODEL), BF16),
            pltpu.VMEM((tm, D_MODEL), F32),
            pltpu.VMEM((tm + SUBLANES, fc), F32),
            pltpu.VMEM((nj, SUBLANES, fc), F32),
            pltpu.VMEM((N_HEADS, HEAD_DIM, PAGE_SIZE), F32),
            pltpu.VMEM((N_HEADS, 1, 1), F32),
            pltpu.VMEM((N_HEADS, 1, PAGE_SIZE), F32),
            pltpu.VMEM((N_HEADS, HEAD_DIM, PAGE_SIZE), F32),
            pltpu.VMEM((N_HEADS, 1), F32),
        ],
    )
    return pl.pallas_call(
        kern,
        grid_spec=grid_spec,
        out_shape=[
            jax.ShapeDtypeStruct((M, D_MODEL), F32),
            jax.ShapeDtypeStruct((M // tm, 2, D_FF), F32),
            jax.ShapeDtypeStruct((n_s, 1, D_ATT), BF16),
        ],
        compiler_params=pltpu.CompilerParams(
            dimension_semantics=("arbitrary", "arbitrary"), vmem_limit_bytes=VMEM_LIMIT),
        name="prompt_mix_ffn",
    )(page_table, x, ya, yr, wo, l1g, l1b, wu, wu, cw, cb, wd.reshape(nj, fc, D_MODEL), l2g, l2b,
      qkvT, lf_new, g_col,
      *([cache_kT] * P), *([cache_vT] * P), *([cache_lfT] * P))


def _sample_mix_ffn_kernel(x_ref, ya_ref, yr_ref, wo_ref, l1g_ref, l1b_ref, wug_ref, wuv_ref, p0_ref,
                           p1_ref, cw_ref, cb_ref, wd_ref, l2g_ref, l2b_ref,
                           o_ref, gate_ref,
                           x1_scr, x1b_scr, acc_scr, *, nj):
    j = pl.program_id(0)

    @pl.when(j == 0)
    def _():
        _mix_head(x_ref, ya_ref, yr_ref, wo_ref, l1g_ref, l1b_ref, x1_scr, x1b_scr, acc_scr)

    x1b = x1b_scr[...]
    gate = _dot(x1b, wug_ref[...])
    val = _dot(x1b, wuv_ref[...])
    gate_ref[...] = gate
    gc = (cb_ref[...] + cw_ref[2:3, :] * gate + cw_ref[1:2, :] * p1_ref[...]
          + cw_ref[0:1, :] * p0_ref[...])
    hid = (_gelu(gc) * val).astype(BF16)
    acc_scr[...] += _dot(hid, wd_ref[...])

    @pl.when(j == nj - 1)
    def _():
        o_ref[...] = _layer_norm(DN_ALPHA * x1_scr[...] + acc_scr[...], l2g_ref[...], l2b_ref[...])


def _sample_mix_ffn(x, ya, yr, wo, l1g, l1b, wu, p0, p1, cw, cb, wd, l2g, l2b):
    n = x.shape[0]
    fc = FC
    nj = D_FF // fc
    kern = functools.partial(_sample_mix_ffn_kernel, nj=nj)
    const = lambda j: (0, 0)
    chunk = lambda j: (0, j)
    return pl.pallas_call(
        kern,
        grid=(nj,),
        in_specs=[
            pl.BlockSpec((n, D_MODEL), const),
            pl.BlockSpec((n, D_ATT), const),
            pl.BlockSpec((n, D_RNN), const),
            pl.BlockSpec((D_MODEL, D_MODEL), const),
            pl.BlockSpec((1, D_MODEL), const),
            pl.BlockSpec((1, D_MODEL), const),
            pl.BlockSpec((D_MODEL, fc), chunk),
            pl.BlockSpec((D_MODEL, fc), lambda j: (0, j + nj)),
            pl.BlockSpec((n, fc), chunk),
            pl.BlockSpec((n, fc), chunk),
            pl.BlockSpec((3, fc), chunk),
            pl.BlockSpec((1, fc), chunk),
            pl.BlockSpec((fc, D_MODEL), lambda j: (j, 0)),
            pl.BlockSpec((1, D_MODEL), const),
            pl.BlockSpec((1, D_MODEL), const),
        ],
        out_specs=[
            pl.BlockSpec((n, D_MODEL), const),
            pl.BlockSpec((n, fc), chunk),
        ],
        out_shape=[
            jax.ShapeDtypeStruct((n, D_MODEL), F32),
            jax.ShapeDtypeStruct((n, D_FF), F32),
        ],
        scratch_shapes=[
            pltpu.VMEM((n, D_MODEL), F32),
            pltpu.VMEM((n, D_MODEL), BF16),
            pltpu.VMEM((n, D_MODEL), F32),
        ],
        compiler_params=pltpu.CompilerParams(
            dimension_semantics=("arbitrary",), vmem_limit_bytes=VMEM_LIMIT),
        name="sample_mix_ffn",
    )(x, ya, yr, wo, l1g, l1b, wu, wu, p0, p1, cw, cb, wd, l2g, l2b)


def _sample_inproj_kernel(x_ref, wn_ref, wt_ref, bf_ref, cs_ref, h0_ref, cw_ref, cb_ref, wg_ref, bg_ref,
                          lam_ref, grnn_ref,
                          qkvT_ref, k_ref, v_ref, lf_ref, yr_ref, h_ref, csn_ref):
    xb = x_ref[...].astype(BF16)
    yn = _dot(xb, wn_ref[...])
    yt = _dot_nt(xb, wt_ref[...])
    k_ref[...] = yt[:, D_ATT:2 * D_ATT]
    v_ref[...] = yt[:, 2 * D_ATT:3 * D_ATT]
    qkvT_ref[...] = _dot_nt(wt_ref[0:3 * D_ATT, :], xb)
    lf_ref[...] = _log_sigmoid(yt[:, 3 * D_ATT:3 * D_ATT + N_HEADS] + bf_ref[...])

    xr = yn[:, D_ATT + LANES:D_ATT + LANES + D_RNN]
    gr = yn[:, D_ATT + LANES + D_RNN:]
    xc = (cb_ref[...] + cw_ref[3:4, :] * xr + cw_ref[2:3, :] * cs_ref[2]
          + cw_ref[1:2, :] * cs_ref[1] + cw_ref[0:1, :] * cs_ref[0])
    csn_ref[0] = cs_ref[1]
    csn_ref[1] = cs_ref[2]
    csn_ref[2] = xr

    a, u = _rglru_gates(xc, wg_ref, bg_ref, lam_ref)
    h = a * h0_ref[...] + u
    h_ref[...] = h
    yr_ref[...] = _rms_norm(h * _gelu(gr), grnn_ref[...]).astype(BF16)


def _sample_inproj(x, wn, wt, bf_row, cs, h0, cw, cb, wg, bg, lam, grnn):
    n = x.shape[0]
    return pl.pallas_call(
        _sample_inproj_kernel,
        out_shape=[
            jax.ShapeDtypeStruct((3 * D_ATT, n), F32),
            jax.ShapeDtypeStruct((n, D_ATT), F32),
            jax.ShapeDtypeStruct((n, D_ATT), F32),
            jax.ShapeDtypeStruct((n, N_HEADS), F32),
            jax.ShapeDtypeStruct((n, D_RNN), BF16),
            jax.ShapeDtypeStruct((n, D_RNN), F32),
            jax.ShapeDtypeStruct((3, n, D_RNN), F32),
        ],
        compiler_params=pltpu.CompilerParams(vmem_limit_bytes=VMEM_LIMIT),
        name="sample_inproj",
    )(x, wn, wt, bf_row, cs, h0, cw, cb, wg, bg, lam, grnn)


def _column_on_lanes(mat, i):
    n = mat.shape[1]
    onehot = (lax.broadcasted_iota(jnp.int32, (n, LANES), 0) == i).astype(BF16)
    hi, mid, lo = _split3(mat)
    return (_dot(hi.astype(BF16), onehot) + _dot(mid.astype(BF16), onehot)
            + _dot(lo.astype(BF16), onehot))


def _sample_attn_init(i, qkvT_ref, qb_scr, m_scr, l_scr, acc_scr, c_scr):
    q = _column_on_lanes(qkvT_ref[0:D_ATT, :], i) * ATT_SCALE
    qb_scr[...] = q.reshape(N_HEADS, HEAD_DIM, PAGE_SIZE)
    m_scr[...] = jnp.full(m_scr.shape, NEG_BIG, F32)
    l_scr[...] = jnp.zeros(l_scr.shape, F32)
    acc_scr[...] = jnp.zeros(acc_scr.shape, F32)
    c_scr[...] = jnp.zeros(c_scr.shape, F32)


def _sample_attn_pages(k_refs, v_refs, lf_refs, qb_scr, m_scr, l_scr, acc_scr, c_scr):
    n_pages = len(k_refs)
    cs, run = _running_sum_blocks([lf_refs[i][...] for i in range(n_pages)], c_scr[...])
    c_scr[...] = run
    qb = qb_scr[...]
    scores = []
    for i in range(n_pages):
        s_i = jnp.sum(k_refs[i][...] * qb, axis=1, keepdims=True)
        scores.append(s_i - cs[i].reshape(N_HEADS, 1, PAGE_SIZE))
    s = jnp.concatenate(scores, axis=-1)

    m = m_scr[...]
    m_new = jnp.maximum(m, jnp.max(s, axis=-1, keepdims=True))
    alpha = jnp.exp(m - m_new)
    p = jnp.exp(s - m_new)
    l_new = alpha * l_scr[...]
    acc = alpha * acc_scr[...]
    for i in range(n_pages):
        p_i = p[:, :, i * PAGE_SIZE:(i + 1) * PAGE_SIZE]
        l_new = l_new + p_i
        acc = acc + v_refs[i][...] * p_i
    l_scr[...] = l_new
    acc_scr[...] = acc
    m_scr[...] = m_new


def _sample_attn_finish(i, qkvT_ref, lfn_ref, g_ref, o_ref, qb_scr, m_scr, l_scr, acc_scr, c_scr):
    hds = (N_HEADS, HEAD_DIM, LANES)
    c_new = (c_scr[...] + lfn_ref[...]).reshape(N_HEADS, 1, 1)
    kn = _column_on_lanes(qkvT_ref[D_ATT:2 * D_ATT, :], i).reshape(hds)
    vn = _column_on_lanes(qkvT_ref[2 * D_ATT:3 * D_ATT, :], i).reshape(hds)[:, :, 0:1]
    s_new = jnp.sum(qb_scr[...] * kn, axis=1, keepdims=True)[:, :, 0:1] - c_new
    m_old = m_scr[...]
    m_fin = jnp.maximum(m_old, s_new)
    a2 = jnp.exp(m_old - m_fin)
    p_new = jnp.exp(s_new - m_fin)
    l_fin = a2 * jnp.sum(l_scr[...], axis=-1, keepdims=True) + p_new
    o3 = (a2 * jnp.sum(acc_scr[...], axis=-1, keepdims=True) + p_new * vn) / l_fin
    y = o3.reshape(D_ATT, 1)
    ms = jnp.mean(y * y, axis=0, keepdims=True)
    yn = (y * lax.rsqrt(ms + EPS) * g_ref[...]).astype(BF16)
    e0 = (lax.broadcasted_iota(jnp.int32, (SUBLANES, LANES), 1) == 0).astype(BF16)
    row = _dot_nt(e0, jnp.broadcast_to(yn, (D_ATT, LANES)))
    o_ref[...] = row[0:1, :].astype(BF16)


def _block_diag(w):
    n, c, d = w.shape
    return jnp.einsum("ncd,nm->ncmd", w, jnp.eye(n, dtype=w.dtype)).reshape(n * c, n * d)


def kernel(x_prompt, x_sample, cache_k, cache_v, cache_logf, page_table, state_h, state_conv_rnn,
           state_conv_ffn, w_in, b_f, rnn_conv_w, rnn_conv_b, w_a, b_a, w_x, b_x, lam, g_att, g_rnn,
           w_out, ln1_g, ln1_b, w_up, ffn_conv_w, ffn_conv_b, w_down, ln2_g, ln2_b):
    B, T, _ = x_prompt.shape
    n_s = x_sample.shape[0]

    cache_kT = jnp.transpose(cache_k, (0, 1, 3, 4, 2))
    cache_vT = jnp.transpose(cache_v, (0, 1, 3, 4, 2))
    cache_lfT = jnp.transpose(cache_logf, (0, 1, 3, 2))

    x_p = x_prompt
    x_s = x_sample.reshape(n_s, D_MODEL)
    outs = [[] for _ in range(12)]
    stacked = None
    c1, c2, c3 = D_ATT, 2 * D_ATT, 3 * D_ATT
    c4 = c3 + N_HEADS
    c5 = c4 + D_RNN
    for l in range(DEPTH):
        wi = w_in[l]
        wn = jnp.concatenate([wi[:, c1:c2], wi[:, c3:c4], jnp.zeros((D_MODEL, LANES - N_HEADS), F32),
                              wi[:, c4:c5], wi[:, c5:]], axis=1).astype(BF16)
        wt = jnp.concatenate([wi[:, :c4], jnp.zeros((D_MODEL, 8), F32)], axis=1).T.astype(BF16)
        bf_row = jnp.concatenate([b_f[l], jnp.zeros((LANES - N_HEADS,), F32)])[None, :]
        wg = jnp.concatenate([_block_diag(w_a[l]), _block_diag(w_x[l])], axis=1).astype(BF16)
        bg = jnp.concatenate([b_a[l], b_x[l]])[None, :]
        wo = w_out[l].astype(BF16)
        wu = w_up[l].astype(BF16)
        wd = w_down[l].astype(BF16)
        row = lambda v: v[None, :]
        cw, cb = rnn_conv_w[l], row(rnn_conv_b[l])
        lam_l, grnn, g_col = row(lam[l]), row(g_rnn[l]), g_att[l][:, None]
        l1g, l1b, l2g, l2b = row(ln1_g[l]), row(ln1_b[l]), row(ln2_g[l]), row(ln2_b[l])
        fcw, fcb = ffn_conv_w[l], row(ffn_conv_b[l])

        cs = jnp.transpose(state_conv_rnn[l], (1, 0, 2))
        qkvT, ks, vs, lfs, yrs, hs, csn = _sample_inproj(
            x_s, wn, wt, b_f[l][None, :], cs, state_h[l], cw, cb, wg, bg, lam_l, grnn)

        qblk, kT_all, vT_all, kblk, vblk, lfT_all, yr, h_last, cr = _prompt_inproj(
            l, x_p, wn, wt, b_f[l][:, None], bf_row, cw, cb, wg, bg, lam_l, grnn, stacked)
        stacked = (kT_all, vT_all, lfT_all)
        ya = _prompt_attn(qblk, kblk, vblk, g_col)
        x2, cf, yas = _prompt_mix_ffn(
            l, x_p.reshape(B * T, D_MODEL), ya.reshape(B * T, D_ATT), yr.reshape(B * T, D_RNN),
            wo, l1g, l1b, wu, fcw, fcb, wd, l2g, l2b, T,
            page_table, qkvT, lfs.reshape(n_s, N_HEADS, 1), g_col,
            cache_kT, cache_vT, cache_lfT)
        x_p = x2.reshape(B, T, D_MODEL)
        outs[3].append(h_last.reshape(B, D_RNN))
        outs[4].append(cr)
        tiles_per_seq = T // TM_FFN
        outs[5].append(cf[tiles_per_seq - 1::tiles_per_seq])

        p0 = state_conv_ffn[l][:, 0, :]
        p1 = state_conv_ffn[l][:, 1, :]
        x_s, gate_s = _sample_mix_ffn(x_s, yas.reshape(n_s, D_ATT), yrs, wo, l1g, l1b, wu, p0, p1,
                                      fcw, fcb, wd, l2g, l2b)
        outs[6].append(ks.reshape(n_s, 1, N_HEADS, HEAD_DIM))
        outs[7].append(vs.reshape(n_s, 1, N_HEADS, HEAD_DIM))
        outs[8].append(lfs.reshape(n_s, 1, N_HEADS))
        outs[9].append(hs)
        outs[10].append(jnp.transpose(csn, (1, 0, 2)))
        outs[11].append(jnp.stack([p1, gate_s], axis=1))

    st = [jnp.stack(o) for o in outs[3:]]
    kT_all, vT_all, lfT_all = stacked
    k_p = kT_all.reshape(DEPTH, B, N_HEADS, HEAD_DIM, T).transpose(0, 1, 4, 2, 3)
    v_p = vT_all.reshape(DEPTH, B, N_HEADS, HEAD_DIM, T).transpose(0, 1, 4, 2, 3)
    lf_p = lfT_all.transpose(0, 1, 3, 2)
    return (x_p, x_s.reshape(n_s, 1, D_MODEL), k_p, v_p, lf_p, *st)
```

```python
import functools

import jax
import jax.numpy as jnp
from jax import lax
from jax.experimental import pallas as pl
from jax.experimental.pallas import tpu as pltpu

F32 = jnp.float32
BF16 = jnp.bfloat16

D_MODEL = 1024
D_ATT = 512
D_RNN = 512
N_HEADS = 8
HEAD_DIM = 64
N_RNN_BLOCKS = 8
D_FF = 3072
RGLRU_C = 8.0
DEPTH = 2
PAGE_SIZE = 128
DN_ALPHA = (2 * DEPTH) ** 0.25
EPS = 1e-5
ATT_SCALE = HEAD_DIM ** -0.5
NEG_BIG = -1e30

LANES = 128
SUBLANES = 8
VMEM_LIMIT = 56 * 1024 * 1024

TM_PROJ = 512
TQ = 256
TK = 128
TM_FFN = 512
FC = 768


def _softplus(x):
    return jnp.maximum(x, 0.0) + jnp.log1p(jnp.exp(-jnp.abs(x)))


def _log_sigmoid(x):
    return -_softplus(-x)


def _gelu(x):
    c = (2.0 / jnp.pi) ** 0.5
    return 0.5 * x * (1.0 + jnp.tanh(c * (x + 0.044715 * (x * x * x))))


def _layer_norm(x, g, b):
    mu = jnp.mean(x, axis=-1, keepdims=True)
    xc = x - mu
    var = jnp.mean(xc * xc, axis=-1, keepdims=True)
    return xc * lax.rsqrt(var + EPS) * g + b


def _rms_norm(x, g):
    return x * lax.rsqrt(jnp.mean(x * x, axis=-1, keepdims=True) + EPS) * g


def _dot(a, b):
    return jnp.dot(a, b, preferred_element_type=F32)


def _dot_nt(a, b):
    return lax.dot_general(a, b, (((1,), (1,)), ((), ())), preferred_element_type=F32)


def _prefix_sum_lanes(x, tri):
    hi = x.astype(BF16)
    r1 = x - hi.astype(F32)
    mid = r1.astype(BF16)
    lo = (r1 - mid.astype(F32)).astype(BF16)
    return _dot(hi, tri) + _dot(mid, tri) + _dot(lo, tri)


def _tri128():
    r = lax.broadcasted_iota(jnp.int32, (LANES, LANES), 0)
    c = lax.broadcasted_iota(jnp.int32, (LANES, LANES), 1)
    return (r <= c).astype(BF16)


def _running_sum_blocks(blocks, run):
    tri = _tri128()
    pre = [_prefix_sum_lanes(b, tri) for b in blocks]
    out = []
    for p in pre:
        out.append(p + run)
        run = run + p[:, LANES - 1:LANES]
    return out, run


def _split3(x):
    hi = x.astype(BF16).astype(F32)
    r1 = x - hi
    mid = r1.astype(BF16).astype(F32)
    lo = (r1 - mid).astype(BF16).astype(F32)
    return hi, mid, lo


def _rglru_gates(xc, wg_ref, bg_ref, lam_ref):
    g = _dot(xc.astype(BF16), wg_ref[...]) + bg_ref[...]
    r = jax.nn.sigmoid(g[:, :D_RNN])
    i = jax.nn.sigmoid(g[:, D_RNN:])
    log_a = (-RGLRU_C * _softplus(-lam_ref[...])) * r
    a = jnp.exp(log_a)
    mult = jnp.sqrt(-jnp.tanh(log_a) * (a * a + 1.0))
    return a, mult * (i * xc)


def _prompt_inproj_kernel(*refs, tm, tq, tk, n_alias):
    (x_ref, wn_ref, wt_ref, bfc_ref, bfr_ref, cw_ref, cb_ref, wg_ref, bg_ref, lam_ref,
     grnn_ref) = refs[:11]
    (qblk_ref, kT_ref, vT_ref, kblk_ref, vblk_ref, lfT_ref, yr_ref, hl_ref, cr_ref,
     xbuf, a_scr, u_scr, h_scr, hc_scr, crun_scr) = refs[11 + n_alias:]
    t = pl.program_id(1)

    @pl.when(t == 0)
    def _():
        xbuf[pl.ds(0, SUBLANES), :] = jnp.zeros((SUBLANES, D_RNN), F32)
        hc_scr[...] = jnp.zeros((SUBLANES, D_RNN), F32)
        crun_scr[...] = jnp.zeros(crun_scr.shape, F32)

    xb = x_ref[...].astype(BF16)
    yn = _dot(xb, wn_ref[...])
    yt = _dot_nt(wt_ref[...], xb)

    qT = yt[0:D_ATT] * ATT_SCALE
    kT = yt[D_ATT:2 * D_ATT]
    vT = yt[2 * D_ATT:3 * D_ATT]
    kT_ref[...] = kT
    vT_ref[...] = vT
    lfT_ref[...] = _log_sigmoid(yt[3 * D_ATT:3 * D_ATT + N_HEADS] + bfc_ref[...])

    lf_rows = _log_sigmoid(yn[:, D_ATT:D_ATT + LANES] + bfr_ref[...])
    ri = lax.broadcasted_iota(jnp.int32, (LANES, LANES), 0)
    ci = lax.broadcasted_iota(jnp.int32, (LANES, LANES), 1)
    lower = (ci <= ri).astype(BF16)
    run = crun_scr[0:1, :]
    c_rows = []
    for j in range(tm // LANES):
        hi, mid, lo = _split3(lf_rows[j * LANES:(j + 1) * LANES])
        pre = (_dot(lower, hi.astype(BF16)) + _dot(lower, mid.astype(BF16))
               + _dot(lower, lo.astype(BF16)))
        c_rows.append(pre + run)
        run = run + pre[LANES - 1:LANES, :]
    crun_scr[...] = jnp.broadcast_to(run, crun_scr.shape)
    nhi, nmid, nlo = _split3(-jnp.concatenate(c_rows, axis=0))

    lane = lax.broadcasted_iota(jnp.int32, (tm, LANES), 1)
    is_head = lane < N_HEADS
    packed = jnp.where(is_head, nhi, 0.0)
    packed = packed + pltpu.roll(jnp.where(is_head, nmid, 0.0), N_HEADS, 1)
    packed = packed + pltpu.roll(jnp.where(is_head, nlo, 0.0), 2 * N_HEADS, 1)
    src = lax.broadcasted_iota(jnp.int32, (LANES, N_HEADS * LANES), 0)
    dst = lax.broadcasted_iota(jnp.int32, (LANES, N_HEADS * LANES), 1)
    src_head = src % N_HEADS
    src_term = src // N_HEADS
    base = jnp.where(src_head % 2 == 0, HEAD_DIM, 0)
    place = ((dst // LANES == src_head) & (dst % LANES == base + src_term)
             & (src_term < 3)).astype(BF16)
    bias_lanes = _dot(packed.astype(BF16), place)

    ones = jnp.ones((HEAD_DIM, tk), BF16)
    rowi = lax.broadcasted_iota(jnp.int32, (HEAD_DIM, tq), 0)
    sel_rows = jnp.where(rowi < 3, 1.0, 0.0).astype(BF16)
    for h in range(N_HEADS):
        pair = yn[:, (h // 2) * LANES:(h // 2 + 1) * LANES]
        own = (lane < HEAD_DIM) if h % 2 == 0 else (lane >= HEAD_DIM)
        kb = (jnp.where(own, pair, 0.0) + bias_lanes[:, h * LANES:(h + 1) * LANES]).astype(BF16)
        chan = slice(0, HEAD_DIM) if h % 2 == 0 else slice(HEAD_DIM, 2 * HEAD_DIM)
        rest = slice(HEAD_DIM, 2 * HEAD_DIM) if h % 2 == 0 else slice(0, HEAD_DIM)
        hrows = slice(h * HEAD_DIM, (h + 1) * HEAD_DIM)
        for c in range(tm // tk):
            cols = slice(c * tk, (c + 1) * tk)
            kblk_ref[c, h] = kb[cols, :]
            vblk_ref[c, h, chan, :] = vT[hrows, cols].astype(BF16)
            vblk_ref[c, h, rest, :] = ones
        for c in range(tm // tq):
            cols = slice(c * tq, (c + 1) * tq)
            qblk_ref[c, h, chan, :] = qT[hrows, cols].astype(BF16)
            qblk_ref[c, h, rest, :] = sel_rows

    xr = yn[:, D_ATT + LANES:D_ATT + LANES + D_RNN]
    gr = yn[:, D_ATT + LANES + D_RNN:]
    xbuf[pl.ds(SUBLANES, tm), :] = xr
    xc = (cb_ref[...] + cw_ref[3:4, :] * xr
          + cw_ref[2:3, :] * xbuf[pl.ds(SUBLANES - 1, tm), :]
          + cw_ref[1:2, :] * xbuf[pl.ds(SUBLANES - 2, tm), :]
          + cw_ref[0:1, :] * xbuf[pl.ds(SUBLANES - 3, tm), :])
    tail = xbuf[pl.ds(tm, SUBLANES), :]
    xbuf[pl.ds(0, SUBLANES), :] = tail
    cr_ref[...] = tail[SUBLANES - 3:, :]

    a, u = _rglru_gates(xc, wg_ref, bg_ref, lam_ref)
    a_scr[...] = a
    u_scr[...] = u

    sub = lax.broadcasted_iota(jnp.int32, (SUBLANES, D_RNN), 0)

    def group(g, hin):
        r0 = g * SUBLANES
        ag = a_scr[pl.ds(r0, SUBLANES), :]
        ug = u_scr[pl.ds(r0, SUBLANES), :]
        for d in (1, 2, 4):
            keep = sub >= d
            a_sh = pltpu.roll(ag, d, 0)
            u_sh = pltpu.roll(ug, d, 0)
            ug = jnp.where(keep, ag * u_sh + ug, ug)
            ag = jnp.where(keep, ag * a_sh, ag)
        hg = ag * hin + ug
        h_scr[pl.ds(r0, SUBLANES), :] = hg
        return jnp.broadcast_to(hg[SUBLANES - 1:SUBLANES, :], (SUBLANES, D_RNN))

    hin = hc_scr[...]
    for g in range(tm // SUBLANES):
        hin = group(g, hin)
    hc_scr[...] = hin
    hl_ref[...] = hin[0:1, :]

    y = h_scr[...] * _gelu(gr)
    yr_ref[...] = _rms_norm(y, grnn_ref[...]).astype(BF16)


def _prompt_inproj(layer, x, wn, wt, bf_col, bf_row, cw, cb, wg, bg, lam, grnn, stacked):
    B, T, _ = x.shape
    tm, tq, tk = TM_PROJ, TQ, TK
    nt = T // tm
    wt_rows = wt.shape[0]
    wn_cols = wn.shape[1]
    const = lambda b, t: (0, 0)
    n_alias = 0 if stacked is None else 3
    kern = functools.partial(_prompt_inproj_kernel, tm=tm, tq=tq, tk=tk, n_alias=n_alias)
    in_specs = [
        pl.BlockSpec((None, tm, D_MODEL), lambda b, t: (b, t, 0)),
        pl.BlockSpec((D_MODEL, wn_cols), const),
        pl.BlockSpec((wt_rows, D_MODEL), const),
        pl.BlockSpec((N_HEADS, 1), const),
        pl.BlockSpec((1, LANES), const),
        pl.BlockSpec((4, D_RNN), const),
        pl.BlockSpec((1, D_RNN), const),
        pl.BlockSpec((D_RNN, 2 * D_RNN), const),
        pl.BlockSpec((1, 2 * D_RNN), const),
        pl.BlockSpec((1, D_RNN), const),
        pl.BlockSpec((1, D_RNN), const),
    ] + [pl.BlockSpec(memory_space=pl.ANY)] * n_alias
    args = (x, wn, wt, bf_col, bf_row, cw, cb, wg, bg, lam, grnn)
    args += () if stacked is None else tuple(stacked)
    q_blk = (None, tm // tq, N_HEADS, 2 * HEAD_DIM, tq)
    v_blk = (None, tm // tk, N_HEADS, 2 * HEAD_DIM, tk)
    k_blk = (None, tm // tk, N_HEADS, tk, 2 * HEAD_DIM)
    blk_map = lambda b, t: (b, t, 0, 0, 0)
    return pl.pallas_call(
        kern,
        grid=(B, nt),
        in_specs=in_specs,
        out_specs=[
            pl.BlockSpec(q_blk, blk_map),
            pl.BlockSpec((None, None, D_ATT, tm), lambda b, t: (layer, b, 0, t)),
            pl.BlockSpec((None, None, D_ATT, tm), lambda b, t: (layer, b, 0, t)),
            pl.BlockSpec(k_blk, blk_map),
            pl.BlockSpec(v_blk, blk_map),
            pl.BlockSpec((None, None, N_HEADS, tm), lambda b, t: (layer, b, 0, t)),
            pl.BlockSpec((None, tm, D_RNN), lambda b, t: (b, t, 0)),
            pl.BlockSpec((None, 1, D_RNN), lambda b, t: (b, 0, 0)),
            pl.BlockSpec((None, 3, D_RNN), lambda b, t: (b, 0, 0)),
        ],
        out_shape=[
            jax.ShapeDtypeStruct((B, T // tq, N_HEADS, 2 * HEAD_DIM, tq), BF16),
            jax.ShapeDtypeStruct((DEPTH, B, D_ATT, T), F32),
            jax.ShapeDtypeStruct((DEPTH, B, D_ATT, T), F32),
            jax.ShapeDtypeStruct((B, T // tk, N_HEADS, tk, 2 * HEAD_DIM), BF16),
            jax.ShapeDtypeStruct((B, T // tk, N_HEADS, 2 * HEAD_DIM, tk), BF16),
            jax.ShapeDtypeStruct((DEPTH, B, N_HEADS, T), F32),
            jax.ShapeDtypeStruct((B, T, D_RNN), BF16),
            jax.ShapeDtypeStruct((B, 1, D_RNN), F32),
            jax.ShapeDtypeStruct((B, 3, D_RNN), F32),
        ],
        scratch_shapes=[
            pltpu.VMEM((tm + SUBLANES, D_RNN), F32),
            pltpu.VMEM((tm, D_RNN), F32),
            pltpu.VMEM((tm, D_RNN), F32),
            pltpu.VMEM((tm, D_RNN), F32),
            pltpu.VMEM((SUBLANES, D_RNN), F32),
            pltpu.VMEM((N_HEADS, LANES), F32),
        ],
        input_output_aliases={} if stacked is None else {11: 1, 12: 2, 13: 5},
        compiler_params=pltpu.CompilerParams(
            dimension_semantics=("arbitrary", "arbitrary"), vmem_limit_bytes=VMEM_LIMIT),
        name="prompt_inproj",
    )(*args)


def _prompt_attn_kernel(q_ref, k_ref, v_ref, g_ref, o_ref, m_scr, acc_scr, *, tq, tk):
    qi = pl.program_id(1)
    kpq = tq // tk
    key = lax.broadcasted_iota(jnp.int32, (tk, tq), 0)
    qry = lax.broadcasted_iota(jnp.int32, (tk, tq), 1)

    for h in range(N_HEADS):
        m_scr[h] = jnp.full((1, tq), NEG_BIG, F32)
        acc_scr[h] = jnp.zeros((LANES, tq), F32)

    def tile(kj, diag):
        for h in range(N_HEADS):
            s = _dot(k_ref[kj, h], q_ref[h])
            if diag is not None:
                s = jnp.where(key + diag * tk <= qry, s, NEG_BIG)
            m = m_scr[h]
            m_new = jnp.maximum(m, jnp.max(s, axis=0, keepdims=True))
            alpha = jnp.exp(m - m_new)
            p = jnp.exp(s - m_new).astype(BF16)
            acc_scr[h] = alpha * acc_scr[h] + _dot(v_ref[kj, h], p)
            m_scr[h] = m_new

    def full_tiles(j, carry, n):
        for d in range(n):
            tile(j * n + d, None)
        return carry

    n_full = qi * kpq
    lax.fori_loop(0, n_full // (2 * kpq), functools.partial(full_tiles, n=2 * kpq), 0)
    lax.fori_loop(n_full // kpq - qi % 2, n_full // kpq, functools.partial(full_tiles, n=kpq), 0)
    for d in range(kpq):
        tile(qi * kpq + d, d)

    halves = []
    for h in range(N_HEADS):
        acc = acc_scr[h]
        if h % 2 == 0:
            halves.append(acc[:HEAD_DIM] / acc[HEAD_DIM:])
        else:
            halves.append(acc[HEAD_DIM:] / acc[:HEAD_DIM])
    yT = jnp.concatenate(halves, axis=0)
    ms = jnp.mean(yT * yT, axis=0, keepdims=True)
    ynT = (yT * lax.rsqrt(ms + EPS) * g_ref[...]).astype(BF16)
    er = lax.broadcasted_iota(jnp.int32, (tq, tq), 0)
    ec = lax.broadcasted_iota(jnp.int32, (tq, tq), 1)
    o_ref[...] = _dot_nt((er == ec).astype(BF16), ynT).astype(BF16)


def _prompt_attn(qblk, kblk, vblk, g_col):
    B, nq, _, _, tq = qblk.shape
    nk, tk = kblk.shape[1], kblk.shape[3]
    T = nq * tq
    kern = functools.partial(_prompt_attn_kernel, tq=tq, tk=tk)
    return pl.pallas_call(
        kern,
        grid=(B, nq),
        in_specs=[
            pl.BlockSpec((None, None, N_HEADS, 2 * HEAD_DIM, tq), lambda b, i: (b, i, 0, 0, 0)),
            pl.BlockSpec((None, nk, N_HEADS, tk, 2 * HEAD_DIM), lambda b, i: (b, 0, 0, 0, 0)),
            pl.BlockSpec((None, nk, N_HEADS, 2 * HEAD_DIM, tk), lambda b, i: (b, 0, 0, 0, 0)),
            pl.BlockSpec((D_ATT, 1), lambda b, i: (0, 0)),
        ],
        out_specs=pl.BlockSpec((None, tq, D_ATT), lambda b, i: (b, i, 0)),
        out_shape=jax.ShapeDtypeStruct((B, T, D_ATT), BF16),
        scratch_shapes=[
            pltpu.VMEM((N_HEADS, 1, tq), F32),
            pltpu.VMEM((N_HEADS, LANES, tq), F32),
        ],
        compiler_params=pltpu.CompilerParams(
            dimension_semantics=("arbitrary", "arbitrary"), vmem_limit_bytes=VMEM_LIMIT),
        name="prompt_attn",
    )(qblk, kblk, vblk, g_col)


def _mix_head(x_ref, ya_ref, yr_ref, wo_ref, l1g_ref, l1b_ref, x1_scr, x1b_scr, acc_scr):
    mix = _dot(ya_ref[...], wo_ref[0:D_ATT, :]) + _dot(yr_ref[...], wo_ref[D_ATT:, :])
    x1 = _layer_norm(DN_ALPHA * x_ref[...] + mix, l1g_ref[...], l1b_ref[...])
    x1_scr[...] = x1
    x1b_scr[...] = x1.astype(BF16)
    acc_scr[...] = jnp.zeros(acc_scr.shape, F32)


def _prompt_mix_ffn_kernel(pt_ref, x_ref, ya_ref, yr_ref, wo_ref, l1g_ref, l1b_ref, wug_ref, wuv_ref,
                           cw_ref, cb_ref, wd_ref, l2g_ref, l2b_ref,
                           qkvT_ref, lfns_ref, gcol_ref, *rest,
                           tm, tiles_per_seq, nj, n_pages):
    k_refs = rest[:n_pages]
    v_refs = rest[n_pages:2 * n_pages]
    lf_refs = rest[2 * n_pages:3 * n_pages]
    (o_ref, cf_ref, yas_ref, x1_scr, x1b_scr, acc_scr, gbuf, carry_scr,
     qb_scr, m_scr, l_scr, sacc_scr, c_scr) = rest[3 * n_pages:]
    del pt_ref
    i = pl.program_id(0)
    j = pl.program_id(1)

    @pl.when(j == 0)
    def _():
        _mix_head(x_ref, ya_ref, yr_ref, wo_ref, l1g_ref, l1b_ref, x1_scr, x1b_scr, acc_scr)
        _sample_attn_init(i, qkvT_ref, qb_scr, m_scr, l_scr, sacc_scr, c_scr)

    x1b = x1b_scr[...]
    gate = _dot(x1b, wug_ref[...])
    val = _dot(x1b, wuv_ref[...])

    first = (i % tiles_per_seq) == 0
    prev = jnp.where(first, 0.0, carry_scr[j])
    gbuf[pl.ds(0, SUBLANES), :] = prev
    gbuf[pl.ds(SUBLANES, tm), :] = gate
    gc = (cb_ref[...] + cw_ref[2:3, :] * gate
          + cw_ref[1:2, :] * gbuf[pl.ds(SUBLANES - 1, tm), :]
          + cw_ref[0:1, :] * gbuf[pl.ds(SUBLANES - 2, tm), :])
    tail = gbuf[pl.ds(tm, SUBLANES), :]
    carry_scr[j] = tail
    cf_ref[...] = tail[SUBLANES - 2:, :]

    hid = (_gelu(gc) * val).astype(BF16)
    acc_scr[...] += _dot(hid, wd_ref[j])

    _sample_attn_pages(k_refs, v_refs, lf_refs, qb_scr, m_scr, l_scr, sacc_scr, c_scr)

    @pl.when(j == nj - 1)
    def _():
        o_ref[...] = _layer_norm(DN_ALPHA * x1_scr[...] + acc_scr[...], l2g_ref[...], l2b_ref[...])
        _sample_attn_finish(i, qkvT_ref, lfns_ref, gcol_ref, yas_ref,
                            qb_scr, m_scr, l_scr, sacc_scr, c_scr)


def _prompt_mix_ffn(layer, x, ya, yr, wo, l1g, l1b, wu, cw, cb, wd, l2g, l2b, seq,
                    page_table, qkvT, lf_new, g_col, cache_kT, cache_vT, cache_lfT):
    M = x.shape[0]
    tm, fc = TM_FFN, FC
    nj = D_FF // fc
    tiles_per_seq = seq // tm
    n_s, n_pt = page_table.shape
    assert M // tm == n_s and n_pt % nj == 0, "one sample sequence per row tile, pages split over chunks"
    P = n_pt // nj
    kern = functools.partial(_prompt_mix_ffn_kernel, tm=tm, tiles_per_seq=tiles_per_seq, nj=nj,
                             n_pages=P)
    row = lambda i, j, pt: (i, 0)
    const = lambda i, j, pt: (0, 0)
    seq3 = lambda i, j, pt: (i, 0, 0)

    def page_spec(k, shape):
        nd = len(shape)
        return pl.BlockSpec((None, None) + shape,
                            lambda i, j, pt, k=k: (layer, pt[i, j * P + k]) + (0,) * nd)

    kv_shape = (N_HEADS, HEAD_DIM, PAGE_SIZE)
    in_specs = [
        pl.BlockSpec((tm, D_MODEL), row),
        pl.BlockSpec((tm, D_ATT), row),
        pl.BlockSpec((tm, D_RNN), row),
        pl.BlockSpec((D_MODEL, D_MODEL), const, pipeline_mode=pl.Buffered(1)),
        pl.BlockSpec((1, D_MODEL), const),
        pl.BlockSpec((1, D_MODEL), const),
        pl.BlockSpec((D_MODEL, fc), lambda i, j, pt: (0, j)),
        pl.BlockSpec((D_MODEL, fc), lambda i, j, pt: (0, j + nj)),
        pl.BlockSpec((3, fc), lambda i, j, pt: (0, j)),
        pl.BlockSpec((1, fc), lambda i, j, pt: (0, j)),
        pl.BlockSpec((nj, fc, D_MODEL), lambda i, j, pt: (0, 0, 0), pipeline_mode=pl.Buffered(1)),
        pl.BlockSpec((1, D_MODEL), const),
        pl.BlockSpec((1, D_MODEL), const),
        pl.BlockSpec((3 * D_ATT, n_s), const),
        pl.BlockSpec((None, N_HEADS, 1), seq3),
        pl.BlockSpec((D_ATT, 1), const),
    ]
    in_specs += [page_spec(k, kv_shape) for k in range(P)]
    in_specs += [page_spec(k, kv_shape) for k in range(P)]
    in_specs += [page_spec(k, (N_HEADS, PAGE_SIZE)) for k in range(P)]
    grid_spec = pltpu.PrefetchScalarGridSpec(
        num_scalar_prefetch=1,
        grid=(M // tm, nj),
        in_specs=in_specs,
        out_specs=[
            pl.BlockSpec((tm, D_MODEL), row),
            pl.BlockSpec((None, 2, fc), lambda i, j, pt: (i, 0, j)),
            pl.BlockSpec((None, 1, D_ATT), seq3),
        ],
        scratch_shapes=[
            pltpu.VMEM((tm, D_MODEL), F32),
            pltpu.VMEM((tm, D_MODEL), BF16),
            pltpu.VMEM((tm, D_MODEL), F32),
            pltpu.VMEM((tm + SUBLANES, fc), F32),
            pltpu.VMEM((nj, SUBLANES, fc), F32),
            pltpu.VMEM((N_HEADS, HEAD_DIM, PAGE_SIZE), F32),
            pltpu.VMEM((N_HEADS, 1, 1), F32),
            pltpu.VMEM((N_HEADS, 1, PAGE_SIZE), F32),
            pltpu.VMEM((N_HEADS, HEAD_DIM, PAGE_SIZE), F32),
            pltpu.VMEM((N_HEADS, 1), F32),
        ],
    )
    return pl.pallas_call(
        kern,
        grid_spec=grid_spec,
        out_shape=[
            jax.ShapeDtypeStruct((M, D_MODEL), F32),
            jax.ShapeDtypeStruct((M // tm, 2, D_FF), F32),
            jax.ShapeDtypeStruct((n_s, 1, D_ATT), BF16),
        ],
        compiler_params=pltpu.CompilerParams(
            dimension_semantics=("arbitrary", "arbitrary"), vmem_limit_bytes=VMEM_LIMIT),
        name="prompt_mix_ffn",
    )(page_table, x, ya, yr, wo, l1g, l1b, wu, wu, cw, cb, wd.reshape(nj, fc, D_MODEL), l2g, l2b,
      qkvT, lf_new, g_col,
      *([cache_kT] * P), *([cache_vT] * P), *([cache_lfT] * P))


def _sample_mix_ffn_kernel(x_ref, ya_ref, yr_ref, wo_ref, l1g_ref, l1b_ref, wug_ref, wuv_ref, p0_ref,
                           p1_ref, cw_ref, cb_ref, wd_ref, l2g_ref, l2b_ref,
                           o_ref, gate_ref,
                           x1_scr, x1b_scr, acc_scr, *, nj):
    j = pl.program_id(0)

    @pl.when(j == 0)
    def _():
        _mix_head(x_ref, ya_ref, yr_ref, wo_ref, l1g_ref, l1b_ref, x1_scr, x1b_scr, acc_scr)

    x1b = x1b_scr[...]
    gate = _dot(x1b, wug_ref[...])
    val = _dot(x1b, wuv_ref[...])
    gate_ref[...] = gate
    gc = (cb_ref[...] + cw_ref[2:3, :] * gate + cw_ref[1:2, :] * p1_ref[...]
          + cw_ref[0:1, :] * p0_ref[...])
    hid = (_gelu(gc) * val).astype(BF16)
    acc_scr[...] += _dot(hid, wd_ref[...])

    @pl.when(j == nj - 1)
    def _():
        o_ref[...] = _layer_norm(DN_ALPHA * x1_scr[...] + acc_scr[...], l2g_ref[...], l2b_ref[...])


def _sample_mix_ffn(x, ya, yr, wo, l1g, l1b, wu, p0, p1, cw, cb, wd, l2g, l2b):
    n = x.shape[0]
    fc = FC
    nj = D_FF // fc
    kern = functools.partial(_sample_mix_ffn_kernel, nj=nj)
    const = lambda j: (0, 0)
    chunk = lambda j: (0, j)
    return pl.pallas_call(
        kern,
        grid=(nj,),
        in_specs=[
            pl.BlockSpec((n, D_MODEL), const),
            pl.BlockSpec((n, D_ATT), const),
            pl.BlockSpec((n, D_RNN), const),
            pl.BlockSpec((D_MODEL, D_MODEL), const),
            pl.BlockSpec((1, D_MODEL), const),
            pl.BlockSpec((1, D_MODEL), const),
            pl.BlockSpec((D_MODEL, fc), chunk),
            pl.BlockSpec((D_MODEL, fc), lambda j: (0, j + nj)),
            pl.BlockSpec((n, fc), chunk),
            pl.BlockSpec((n, fc), chunk),
            pl.BlockSpec((3, fc), chunk),
            pl.BlockSpec((1, fc), chunk),
            pl.BlockSpec((fc, D_MODEL), lambda j: (j, 0)),
            pl.BlockSpec((1, D_MODEL), const),
            pl.BlockSpec((1, D_MODEL), const),
        ],
        out_specs=[
            pl.BlockSpec((n, D_MODEL), const),
            pl.BlockSpec((n, fc), chunk),
        ],
        out_shape=[
            jax.ShapeDtypeStruct((n, D_MODEL), F32),
            jax.ShapeDtypeStruct((n, D_FF), F32),
        ],
        scratch_shapes=[
            pltpu.VMEM((n, D_MODEL), F32),
            pltpu.VMEM((n, D_MODEL), BF16),
            pltpu.VMEM((n, D_MODEL), F32),
        ],
        compiler_params=pltpu.CompilerParams(
            dimension_semantics=("arbitrary",), vmem_limit_bytes=VMEM_LIMIT),
        name="sample_mix_ffn",
    )(x, ya, yr, wo, l1g, l1b, wu, wu, p0, p1, cw, cb, wd, l2g, l2b)


def _sample_inproj_kernel(x_ref, wn_ref, wt_ref, bf_ref, cs_ref, h0_ref, cw_ref, cb_ref, wg_ref, bg_ref,
                          lam_ref, grnn_ref,
                          qkvT_ref, k_ref, v_ref, lf_ref, yr_ref, h_ref, csn_ref):
    xb = x_ref[...].astype(BF16)
    yn = _dot(xb, wn_ref[...])
    yt = _dot_nt(xb, wt_ref[...])
    k_ref[...] = yt[:, D_ATT:2 * D_ATT]
    v_ref[...] = yt[:, 2 * D_ATT:3 * D_ATT]
    qkvT_ref[...] = _dot_nt(wt_ref[0:3 * D_ATT, :], xb)
    lf_ref[...] = _log_sigmoid(yt[:, 3 * D_ATT:3 * D_ATT + N_HEADS] + bf_ref[...])

    xr = yn[:, D_ATT + LANES:D_ATT + LANES + D_RNN]
    gr = yn[:, D_ATT + LANES + D_RNN:]
    xc = (cb_ref[...] + cw_ref[3:4, :] * xr + cw_ref[2:3, :] * cs_ref[2]
          + cw_ref[1:2, :] * cs_ref[1] + cw_ref[0:1, :] * cs_ref[0])
    csn_ref[0] = cs_ref[1]
    csn_ref[1] = cs_ref[2]
    csn_ref[2] = xr

    a, u = _rglru_gates(xc, wg_ref, bg_ref, lam_ref)
    h = a * h0_ref[...] + u
    h_ref[...] = h
    yr_ref[...] = _rms_norm(h * _gelu(gr), grnn_ref[...]).astype(BF16)


def _sample_inproj(x, wn, wt, bf_row, cs, h0, cw, cb, wg, bg, lam, grnn):
    n = x.shape[0]
    return pl.pallas_call(
        _sample_inproj_kernel,
        out_shape=[
            jax.ShapeDtypeStruct((3 * D_ATT, n), F32),
            jax.ShapeDtypeStruct((n, D_ATT), F32),
            jax.ShapeDtypeStruct((n, D_ATT), F32),
            jax.ShapeDtypeStruct((n, N_HEADS), F32),
            jax.ShapeDtypeStruct((n, D_RNN), BF16),
            jax.ShapeDtypeStruct((n, D_RNN), F32),
            jax.ShapeDtypeStruct((3, n, D_RNN), F32),
        ],
        compiler_params=pltpu.CompilerParams(vmem_limit_bytes=VMEM_LIMIT),
        name="sample_inproj",
    )(x, wn, wt, bf_row, cs, h0, cw, cb, wg, bg, lam, grnn)


def _column_on_lanes(mat, i):
    n = mat.shape[1]
    onehot = (lax.broadcasted_iota(jnp.int32, (n, LANES), 0) == i).astype(BF16)
    hi, mid, lo = _split3(mat)
    return (_dot(hi.astype(BF16), onehot) + _dot(mid.astype(BF16), onehot)
            + _dot(lo.astype(BF16), onehot))


def _sample_attn_init(i, qkvT_ref, qb_scr, m_scr, l_scr, acc_scr, c_scr):
    q = _column_on_lanes(qkvT_ref[0:D_ATT, :], i) * ATT_SCALE
    qb_scr[...] = q.reshape(N_HEADS, HEAD_DIM, PAGE_SIZE)
    m_scr[...] = jnp.full(m_scr.shape, NEG_BIG, F32)
    l_scr[...] = jnp.zeros(l_scr.shape, F32)
    acc_scr[...] = jnp.zeros(acc_scr.shape, F32)
    c_scr[...] = jnp.zeros(c_scr.shape, F32)


def _sample_attn_pages(k_refs, v_refs, lf_refs, qb_scr, m_scr, l_scr, acc_scr, c_scr):
    n_pages = len(k_refs)
    cs, run = _running_sum_blocks([lf_refs[i][...] for i in range(n_pages)], c_scr[...])
    c_scr[...] = run
    qb = qb_scr[...]
    scores = []
    for i in range(n_pages):
        s_i = jnp.sum(k_refs[i][...] * qb, axis=1, keepdims=True)
        scores.append(s_i - cs[i].reshape(N_HEADS, 1, PAGE_SIZE))
    s = jnp.concatenate(scores, axis=-1)

    m = m_scr[...]
    m_new = jnp.maximum(m, jnp.max(s, axis=-1, keepdims=True))
    alpha = jnp.exp(m - m_new)
    p = jnp.exp(s - m_new)
    l_new = alpha * l_scr[...]
    acc = alpha * acc_scr[...]
    for i in range(n_pages):
        p_i = p[:, :, i * PAGE_SIZE:(i + 1) * PAGE_SIZE]
        l_new = l_new + p_i
        acc = acc + v_refs[i][...] * p_i
    l_scr[...] = l_new
    acc_scr[...] = acc
    m_scr[...] = m_new


def _sample_attn_finish(i, qkvT_ref, lfn_ref, g_ref, o_ref, qb_scr, m_scr, l_scr, acc_scr, c_scr):
    hds = (N_HEADS, HEAD_DIM, LANES)
    c_new = (c_scr[...] + lfn_ref[...]).reshape(N_HEADS, 1, 1)
    kn = _column_on_lanes(qkvT_ref[D_ATT:2 * D_ATT, :], i).reshape(hds)
    vn = _column_on_lanes(qkvT_ref[2 * D_ATT:3 * D_ATT, :], i).reshape(hds)[:, :, 0:1]
    s_new = jnp.sum(qb_scr[...] * kn, axis=1, keepdims=True)[:, :, 0:1] - c_new
    m_old = m_scr[...]
    m_fin = jnp.maximum(m_old, s_new)
    a2 = jnp.exp(m_old - m_fin)
    p_new = jnp.exp(s_new - m_fin)
    l_fin = a2 * jnp.sum(l_scr[...], axis=-1, keepdims=True) + p_new
    o3 = (a2 * jnp.sum(acc_scr[...], axis=-1, keepdims=True) + p_new * vn) / l_fin
    y = o3.reshape(D_ATT, 1)
    ms = jnp.mean(y * y, axis=0, keepdims=True)
    yn = (y * lax.rsqrt(ms + EPS) * g_ref[...]).astype(BF16)
    e0 = (lax.broadcasted_iota(jnp.int32, (SUBLANES, LANES), 1) == 0).astype(BF16)
    row = _dot_nt(e0, jnp.broadcast_to(yn, (D_ATT, LANES)))
    o_ref[...] = row[0:1, :].astype(BF16)


def _block_diag(w):
    n, c, d = w.shape
    return jnp.einsum("ncd,nm->ncmd", w, jnp.eye(n, dtype=w.dtype)).reshape(n * c, n * d)


def kernel(x_prompt, x_sample, cache_k, cache_v, cache_logf, page_table, state_h, state_conv_rnn,
           state_conv_ffn, w_in, b_f, rnn_conv_w, rnn_conv_b, w_a, b_a, w_x, b_x, lam, g_att, g_rnn,
           w_out, ln1_g, ln1_b, w_up, ffn_conv_w, ffn_conv_b, w_down, ln2_g, ln2_b):
    B, T, _ = x_prompt.shape
    n_s = x_sample.shape[0]

    cache_kT = jnp.transpose(cache_k, (0, 1, 3, 4, 2))
    cache_vT = jnp.transpose(cache_v, (0, 1, 3, 4, 2))
    cache_lfT = jnp.transpose(cache_logf, (0, 1, 3, 2))

    x_p = x_prompt
    x_s = x_sample.reshape(n_s, D_MODEL)
    outs = [[] for _ in range(12)]
    stacked = None
    c1, c2, c3 = D_ATT, 2 * D_ATT, 3 * D_ATT
    c4 = c3 + N_HEADS
    c5 = c4 + D_RNN
    for l in range(DEPTH):
        wi = w_in[l]
        wn = jnp.concatenate([wi[:, c1:c2], wi[:, c3:c4], jnp.zeros((D_MODEL, LANES - N_HEADS), F32),
                              wi[:, c4:c5], wi[:, c5:]], axis=1).astype(BF16)
        wt = jnp.concatenate([wi[:, :c4], jnp.zeros((D_MODEL, 8), F32)], axis=1).T.astype(BF16)
        bf_row = jnp.concatenate([b_f[l], jnp.zeros((LANES - N_HEADS,), F32)])[None, :]
        wg = jnp.concatenate([_block_diag(w_a[l]), _block_diag(w_x[l])], axis=1).astype(BF16)
        bg = jnp.concatenate([b_a[l], b_x[l]])[None, :]
        wo = w_out[l].astype(BF16)
        wu = w_up[l].astype(BF16)
        wd = w_down[l].astype(BF16)
        row = lambda v: v[None, :]
        cw, cb = rnn_conv_w[l], row(rnn_conv_b[l])
        lam_l, grnn, g_col = row(lam[l]), row(g_rnn[l]), g_att[l][:, None]
        l1g, l1b, l2g, l2b = row(ln1_g[l]), row(ln1_b[l]), row(ln2_g[l]), row(ln2_b[l])
        fcw, fcb = ffn_conv_w[l], row(ffn_conv_b[l])

        cs = jnp.transpose(state_conv_rnn[l], (1, 0, 2))
        qkvT, ks, vs, lfs, yrs, hs, csn = _sample_inproj(
            x_s, wn, wt, b_f[l][None, :], cs, state_h[l], cw, cb, wg, bg, lam_l, grnn)

        qblk, kT_all, vT_all, kblk, vblk, lfT_all, yr, h_last, cr = _prompt_inproj(
            l, x_p, wn, wt, b_f[l][:, None], bf_row, cw, cb, wg, bg, lam_l, grnn, stacked)
        stacked = (kT_all, vT_all, lfT_all)
        ya = _prompt_attn(qblk, kblk, vblk, g_col)
        x2, cf, yas = _prompt_mix_ffn(
            l, x_p.reshape(B * T, D_MODEL), ya.reshape(B * T, D_ATT), yr.reshape(B * T, D_RNN),
            wo, l1g, l1b, wu, fcw, fcb, wd, l2g, l2b, T,
            page_table, qkvT, lfs.reshape(n_s, N_HEADS, 1), g_col,
            cache_kT, cache_vT, cache_lfT)
        x_p = x2.reshape(B, T, D_MODEL)
        outs[3].append(h_last.reshape(B, D_RNN))
        outs[4].append(cr)
        tiles_per_seq = T // TM_FFN
        outs[5].append(cf[tiles_per_seq - 1::tiles_per_seq])

        p0 = state_conv_ffn[l][:, 0, :]
        p1 = state_conv_ffn[l][:, 1, :]
        x_s, gate_s = _sample_mix_ffn(x_s, yas.reshape(n_s, D_ATT), yrs, wo, l1g, l1b, wu, p0, p1,
                                      fcw, fcb, wd, l2g, l2b)
        outs[6].append(ks.reshape(n_s, 1, N_HEADS, HEAD_DIM))
        outs[7].append(vs.reshape(n_s, 1, N_HEADS, HEAD_DIM))
        outs[8].append(lfs.reshape(n_s, 1, N_HEADS))
        outs[9].append(hs)
        outs[10].append(jnp.transpose(csn, (1, 0, 2)))
        outs[11].append(jnp.stack([p1, gate_s], axis=1))

    st = [jnp.stack(o) for o in outs[3:]]
    kT_all, vT_all, lfT_all = stacked
    k_p = kT_all.reshape(DEPTH, B, N_HEADS, HEAD_DIM, T).transpose(0, 1, 4, 2, 3)
    v_p = vT_all.reshape(DEPTH, B, N_HEADS, HEAD_DIM, T).transpose(0, 1, 4, 2, 3)
    lf_p = lfT_all.transpose(0, 1, 3, 2)
    return (x_p, x_s.reshape(n_s, 1, D_MODEL), k_p, v_p, lf_p, *st)
```

```python
import functools

import jax
import jax.numpy as jnp
from jax import lax
from jax.experimental import pallas as pl
from jax.experimental.pallas import tpu as pltpu

F32 = jnp.float32
BF16 = jnp.bfloat16

D_MODEL = 1024
D_ATT = 512
D_RNN = 512
N_HEADS = 8
HEAD_DIM = 64
N_RNN_BLOCKS = 8
D_FF = 3072
RGLRU_C = 8.0
DEPTH = 2
PAGE_SIZE = 128
DN_ALPHA = (2 * DEPTH) ** 0.25
EPS = 1e-5
ATT_SCALE = HEAD_DIM ** -0.5
NEG_BIG = -1e30

LANES = 128
SUBLANES = 8
VMEM_LIMIT = 56 * 1024 * 1024

TM_PROJ = 1024
TQ = 256
TK = 128
TM_FFN = 512
FC = 768


def _softplus(x):
    return jnp.maximum(x, 0.0) + jnp.log1p(jnp.exp(-jnp.abs(x)))


def _log_sigmoid(x):
    return -_softplus(-x)


def _gelu(x):
    c = (2.0 / jnp.pi) ** 0.5
    return 0.5 * x * (1.0 + jnp.tanh(c * (x + 0.044715 * (x * x * x))))


def _layer_norm(x, g, b):
    mu = jnp.mean(x, axis=-1, keepdims=True)
    xc = x - mu
    var = jnp.mean(xc * xc, axis=-1, keepdims=True)
    return xc * lax.rsqrt(var + EPS) * g + b


def _rms_norm(x, g):
    return x * lax.rsqrt(jnp.mean(x * x, axis=-1, keepdims=True) + EPS) * g


def _dot(a, b):
    return jnp.dot(a, b, preferred_element_type=F32)


def _dot_nt(a, b):
    return lax.dot_general(a, b, (((1,), (1,)), ((), ())), preferred_element_type=F32)


def _prefix_sum_lanes(x, tri):
    hi = x.astype(BF16)
    r1 = x - hi.astype(F32)
    mid = r1.astype(BF16)
    lo = (r1 - mid.astype(F32)).astype(BF16)
    return _dot(hi, tri) + _dot(mid, tri) + _dot(lo, tri)


def _tri128():
    r = lax.broadcasted_iota(jnp.int32, (LANES, LANES), 0)
    c = lax.broadcasted_iota(jnp.int32, (LANES, LANES), 1)
    return (r <= c).astype(BF16)


def _running_sum_blocks(blocks, run):
    tri = _tri128()
    pre = [_prefix_sum_lanes(b, tri) for b in blocks]
    out = []
    for p in pre:
        out.append(p + run)
        run = run + p[:, LANES - 1:LANES]
    return out, run


def _split3(x):
    hi = x.astype(BF16).astype(F32)
    r1 = x - hi
    mid = r1.astype(BF16).astype(F32)
    lo = (r1 - mid).astype(BF16).astype(F32)
    return hi, mid, lo


def _rglru_gates(xc, wg_ref, bg_ref, lam_ref):
    g = _dot(xc.astype(BF16), wg_ref[...]) + bg_ref[...]
    r = jax.nn.sigmoid(g[:, :D_RNN])
    i = jax.nn.sigmoid(g[:, D_RNN:])
    log_a = (-RGLRU_C * _softplus(-lam_ref[...])) * r
    a = jnp.exp(log_a)
    mult = jnp.sqrt(-jnp.tanh(log_a) * (a * a + 1.0))
    return a, mult * (i * xc)


def _prompt_inproj_kernel(*refs, tm, tq, tk, n_alias):
    (x_ref, wn_ref, wt_ref, bfc_ref, bfr_ref, cw_ref, cb_ref, wg_ref, bg_ref, lam_ref,
     grnn_ref) = refs[:11]
    (qblk_ref, kT_ref, vT_ref, kblk_ref, vblk_ref, lfT_ref, yr_ref, hl_ref, cr_ref,
     xbuf, a_scr, u_scr, h_scr, hc_scr, crun_scr) = refs[11 + n_alias:]
    t = pl.program_id(1)

    @pl.when(t == 0)
    def _():
        xbuf[pl.ds(0, SUBLANES), :] = jnp.zeros((SUBLANES, D_RNN), F32)
        hc_scr[...] = jnp.zeros((SUBLANES, D_RNN), F32)
        crun_scr[...] = jnp.zeros(crun_scr.shape, F32)

    xb = x_ref[...].astype(BF16)
    yn = _dot(xb, wn_ref[...])
    yt = _dot_nt(wt_ref[...], xb)

    qT = yt[0:D_ATT] * ATT_SCALE
    kT = yt[D_ATT:2 * D_ATT]
    vT = yt[2 * D_ATT:3 * D_ATT]
    kT_ref[...] = kT
    vT_ref[...] = vT
    lfT_ref[...] = _log_sigmoid(yt[3 * D_ATT:3 * D_ATT + N_HEADS] + bfc_ref[...])

    lf_rows = _log_sigmoid(yn[:, D_ATT:D_ATT + LANES] + bfr_ref[...])
    ri = lax.broadcasted_iota(jnp.int32, (LANES, LANES), 0)
    ci = lax.broadcasted_iota(jnp.int32, (LANES, LANES), 1)
    lower = (ci <= ri).astype(BF16)
    run = crun_scr[0:1, :]
    c_rows = []
    for j in range(tm // LANES):
        hi, mid, lo = _split3(lf_rows[j * LANES:(j + 1) * LANES])
        pre = (_dot(lower, hi.astype(BF16)) + _dot(lower, mid.astype(BF16))
               + _dot(lower, lo.astype(BF16)))
        c_rows.append(pre + run)
        run = run + pre[LANES - 1:LANES, :]
    crun_scr[...] = jnp.broadcast_to(run, crun_scr.shape)
    nhi, nmid, nlo = _split3(-jnp.concatenate(c_rows, axis=0))

    lane = lax.broadcasted_iota(jnp.int32, (tm, LANES), 1)
    is_head = lane < N_HEADS
    packed = jnp.where(is_head, nhi, 0.0)
    packed = packed + pltpu.roll(jnp.where(is_head, nmid, 0.0), N_HEADS, 1)
    packed = packed + pltpu.roll(jnp.where(is_head, nlo, 0.0), 2 * N_HEADS, 1)
    src = lax.broadcasted_iota(jnp.int32, (LANES, N_HEADS * LANES), 0)
    dst = lax.broadcasted_iota(jnp.int32, (LANES, N_HEADS * LANES), 1)
    src_head = src % N_HEADS
    src_term = src // N_HEADS
    base = jnp.where(src_head % 2 == 0, HEAD_DIM, 0)
    place = ((dst // LANES == src_head) & (dst % LANES == base + src_term)
             & (src_term < 3)).astype(BF16)
    bias_lanes = _dot(packed.astype(BF16), place)

    ones = jnp.ones((HEAD_DIM, tk), BF16)
    rowi = lax.broadcasted_iota(jnp.int32, (HEAD_DIM, tq), 0)
    sel_rows = jnp.where(rowi < 3, 1.0, 0.0).astype(BF16)
    for h in range(N_HEADS):
        pair = yn[:, (h // 2) * LANES:(h // 2 + 1) * LANES]
        own = (lane < HEAD_DIM) if h % 2 == 0 else (lane >= HEAD_DIM)
        kb = (jnp.where(own, pair, 0.0) + bias_lanes[:, h * LANES:(h + 1) * LANES]).astype(BF16)
        chan = slice(0, HEAD_DIM) if h % 2 == 0 else slice(HEAD_DIM, 2 * HEAD_DIM)
        rest = slice(HEAD_DIM, 2 * HEAD_DIM) if h % 2 == 0 else slice(0, HEAD_DIM)
        hrows = slice(h * HEAD_DIM, (h + 1) * HEAD_DIM)
        for c in range(tm // tk):
            cols = slice(c * tk, (c + 1) * tk)
            kblk_ref[c, h] = kb[cols, :]
            vblk_ref[c, h, chan, :] = vT[hrows, cols].astype(BF16)
            vblk_ref[c, h, rest, :] = ones
        for c in range(tm // tq):
            cols = slice(c * tq, (c + 1) * tq)
            qblk_ref[c, h, chan, :] = qT[hrows, cols].astype(BF16)
            qblk_ref[c, h, rest, :] = sel_rows

    xr = yn[:, D_ATT + LANES:D_ATT + LANES + D_RNN]
    gr = yn[:, D_ATT + LANES + D_RNN:]
    xbuf[pl.ds(SUBLANES, tm), :] = xr
    xc = (cb_ref[...] + cw_ref[3:4, :] * xr
          + cw_ref[2:3, :] * xbuf[pl.ds(SUBLANES - 1, tm), :]
          + cw_ref[1:2, :] * xbuf[pl.ds(SUBLANES - 2, tm), :]
          + cw_ref[0:1, :] * xbuf[pl.ds(SUBLANES - 3, tm), :])
    tail = xbuf[pl.ds(tm, SUBLANES), :]
    xbuf[pl.ds(0, SUBLANES), :] = tail
    cr_ref[...] = tail[SUBLANES - 3:, :]

    a, u = _rglru_gates(xc, wg_ref, bg_ref, lam_ref)
    a_scr[...] = a
    u_scr[...] = u

    sub = lax.broadcasted_iota(jnp.int32, (SUBLANES, D_RNN), 0)

    def group(g, hin):
        r0 = g * SUBLANES
        ag = a_scr[pl.ds(r0, SUBLANES), :]
        ug = u_scr[pl.ds(r0, SUBLANES), :]
        for d in (1, 2, 4):
            keep = sub >= d
            a_sh = pltpu.roll(ag, d, 0)
            u_sh = pltpu.roll(ug, d, 0)
            ug = jnp.where(keep, ag * u_sh + ug, ug)
            ag = jnp.where(keep, ag * a_sh, ag)
        hg = ag * hin + ug
        h_scr[pl.ds(r0, SUBLANES), :] = hg
        return jnp.broadcast_to(hg[SUBLANES - 1:SUBLANES, :], (SUBLANES, D_RNN))

    hin = hc_scr[...]
    for g in range(tm // SUBLANES):
        hin = group(g, hin)
    hc_scr[...] = hin
    hl_ref[...] = hin[0:1, :]

    y = h_scr[...] * _gelu(gr)
    yr_ref[...] = _rms_norm(y, grnn_ref[...]).astype(BF16)


def _prompt_inproj(layer, x, wn, wt, bf_col, bf_row, cw, cb, wg, bg, lam, grnn, stacked):
    B, T, _ = x.shape
    tm, tq, tk = TM_PROJ, TQ, TK
    nt = T // tm
    wt_rows = wt.shape[0]
    wn_cols = wn.shape[1]
    const = lambda b, t: (0, 0)
    n_alias = 0 if stacked is None else 3
    kern = functools.partial(_prompt_inproj_kernel, tm=tm, tq=tq, tk=tk, n_alias=n_alias)
    in_specs = [
        pl.BlockSpec((None, tm, D_MODEL), lambda b, t: (b, t, 0)),
        pl.BlockSpec((D_MODEL, wn_cols), const),
        pl.BlockSpec((wt_rows, D_MODEL), const),
        pl.BlockSpec((N_HEADS, 1), const),
        pl.BlockSpec((1, LANES), const),
        pl.BlockSpec((4, D_RNN), const),
        pl.BlockSpec((1, D_RNN), const),
        pl.BlockSpec((D_RNN, 2 * D_RNN), const),
        pl.BlockSpec((1, 2 * D_RNN), const),
        pl.BlockSpec((1, D_RNN), const),
        pl.BlockSpec((1, D_RNN), const),
    ] + [pl.BlockSpec(memory_space=pl.ANY)] * n_alias
    args = (x, wn, wt, bf_col, bf_row, cw, cb, wg, bg, lam, grnn)
    args += () if stacked is None else tuple(stacked)
    q_blk = (None, tm // tq, N_HEADS, 2 * HEAD_DIM, tq)
    v_blk = (None, tm // tk, N_HEADS, 2 * HEAD_DIM, tk)
    k_blk = (None, tm // tk, N_HEADS, tk, 2 * HEAD_DIM)
    blk_map = lambda b, t: (b, t, 0, 0, 0)
    return pl.pallas_call(
        kern,
        grid=(B, nt),
        in_specs=in_specs,
        out_specs=[
            pl.BlockSpec(q_blk, blk_map),
            pl.BlockSpec((None, None, D_ATT, tm), lambda b, t: (layer, b, 0, t)),
            pl.BlockSpec((None, None, D_ATT, tm), lambda b, t: (layer, b, 0, t)),
            pl.BlockSpec(k_blk, blk_map),
            pl.BlockSpec(v_blk, blk_map),
            pl.BlockSpec((None, None, N_HEADS, tm), lambda b, t: (layer, b, 0, t)),
            pl.BlockSpec((None, tm, D_RNN), lambda b, t: (b, t, 0)),
            pl.BlockSpec((None, 1, D_RNN), lambda b, t: (b, 0, 0)),
            pl.BlockSpec((None, 3, D_RNN), lambda b, t: (b, 0, 0)),
        ],
        out_shape=[
            jax.ShapeDtypeStruct((B, T // tq, N_HEADS, 2 * HEAD_DIM, tq), BF16),
            jax.ShapeDtypeStruct((DEPTH, B, D_ATT, T), F32),
            jax.ShapeDtypeStruct((DEPTH, B, D_ATT, T), F32),
            jax.ShapeDtypeStruct((B, T // tk, N_HEADS, tk, 2 * HEAD_DIM), BF16),
            jax.ShapeDtypeStruct((B, T // tk, N_HEADS, 2 * HEAD_DIM, tk), BF16),
            jax.ShapeDtypeStruct((DEPTH, B, N_HEADS, T), F32),
            jax.ShapeDtypeStruct((B, T, D_RNN), BF16),
            jax.ShapeDtypeStruct((B, 1, D_RNN), F32),
            jax.ShapeDtypeStruct((B, 3, D_RNN), F32),
        ],
        scratch_shapes=[
            pltpu.VMEM((tm + SUBLANES, D_RNN), F32),
            pltpu.VMEM((tm, D_RNN), F32),
            pltpu.VMEM((tm, D_RNN), F32),
            pltpu.VMEM((tm, D_RNN), F32),
            pltpu.VMEM((SUBLANES, D_RNN), F32),
            pltpu.VMEM((N_HEADS, LANES), F32),
        ],
        input_output_aliases={} if stacked is None else {11: 1, 12: 2, 13: 5},
        compiler_params=pltpu.CompilerParams(
            dimension_semantics=("arbitrary", "arbitrary"), vmem_limit_bytes=VMEM_LIMIT),
        name="prompt_inproj",
    )(*args)


def _prompt_attn_kernel(q_ref, k_ref, v_ref, g_ref, o_ref, m_scr, acc_scr, *, tq, tk):
    qi = pl.program_id(1)
    kpq = tq // tk
    key = lax.broadcasted_iota(jnp.int32, (tk, tq), 0)
    qry = lax.broadcasted_iota(jnp.int32, (tk, tq), 1)

    for h in range(N_HEADS):
        m_scr[h] = jnp.full((1, tq), NEG_BIG, F32)
        acc_scr[h] = jnp.zeros((LANES, tq), F32)

    def tile(kj, diag):
        for h in range(N_HEADS):
            s = _dot(k_ref[kj, h], q_ref[h])
            if diag is not None:
                s = jnp.where(key + diag * tk <= qry, s, NEG_BIG)
            m = m_scr[h]
            m_new = jnp.maximum(m, jnp.max(s, axis=0, keepdims=True))
            alpha = jnp.exp(m - m_new)
            p = jnp.exp(s - m_new).astype(BF16)
            acc_scr[h] = alpha * acc_scr[h] + _dot(v_ref[kj, h], p)
            m_scr[h] = m_new

    def full_tiles(j, carry, n):
        for d in range(n):
            tile(j * n + d, None)
        return carry

    groups = qi
    lax.fori_loop(0, groups // 4, functools.partial(full_tiles, n=4 * kpq), 0)
    lax.fori_loop(2 * (groups // 4), 2 * (groups // 4) + (groups // 2) % 2,
                  functools.partial(full_tiles, n=2 * kpq), 0)
    lax.fori_loop(groups - groups % 2, groups, functools.partial(full_tiles, n=kpq), 0)
    for d in range(kpq):
        tile(qi * kpq + d, d)

    halves = []
    for h in range(N_HEADS):
        acc = acc_scr[h]
        if h % 2 == 0:
            halves.append(acc[:HEAD_DIM] / acc[HEAD_DIM:])
        else:
            halves.append(acc[HEAD_DIM:] / acc[:HEAD_DIM])
    yT = jnp.concatenate(halves, axis=0)
    ms = jnp.mean(yT * yT, axis=0, keepdims=True)
    ynT = (yT * lax.rsqrt(ms + EPS) * g_ref[...]).astype(BF16)
    er = lax.broadcasted_iota(jnp.int32, (tq, tq), 0)
    ec = lax.broadcasted_iota(jnp.int32, (tq, tq), 1)
    o_ref[...] = _dot_nt((er == ec).astype(BF16), ynT).astype(BF16)


def _prompt_attn(qblk, kblk, vblk, g_col):
    B, nq, _, _, tq = qblk.shape
    nk, tk = kblk.shape[1], kblk.shape[3]
    T = nq * tq
    kern = functools.partial(_prompt_attn_kernel, tq=tq, tk=tk)
    return pl.pallas_call(
        kern,
        grid=(B, nq),
        in_specs=[
            pl.BlockSpec((None, None, N_HEADS, 2 * HEAD_DIM, tq), lambda b, i: (b, i, 0, 0, 0)),
            pl.BlockSpec((None, nk, N_HEADS, tk, 2 * HEAD_DIM), lambda b, i: (b, 0, 0, 0, 0)),
            pl.BlockSpec((None, nk, N_HEADS, 2 * HEAD_DIM, tk), lambda b, i: (b, 0, 0, 0, 0)),
            pl.BlockSpec((D_ATT, 1), lambda b, i: (0, 0)),
        ],
        out_specs=pl.BlockSpec((None, tq, D_ATT), lambda b, i: (b, i, 0)),
        out_shape=jax.ShapeDtypeStruct((B, T, D_ATT), BF16),
        scratch_shapes=[
            pltpu.VMEM((N_HEADS, 1, tq), F32),
            pltpu.VMEM((N_HEADS, LANES, tq), F32),
        ],
        compiler_params=pltpu.CompilerParams(
            dimension_semantics=("arbitrary", "arbitrary"), vmem_limit_bytes=VMEM_LIMIT),
        name="prompt_attn",
    )(qblk, kblk, vblk, g_col)


def _mix_head(x_ref, ya_ref, yr_ref, wo_ref, l1g_ref, l1b_ref, x1_scr, x1b_scr, acc_scr):
    mix = _dot(ya_ref[...], wo_ref[0:D_ATT, :]) + _dot(yr_ref[...], wo_ref[D_ATT:, :])
    x1 = _layer_norm(DN_ALPHA * x_ref[...] + mix, l1g_ref[...], l1b_ref[...])
    x1_scr[...] = x1
    x1b_scr[...] = x1.astype(BF16)
    acc_scr[...] = jnp.zeros(acc_scr.shape, F32)


def _prompt_mix_ffn_kernel(pt_ref, x_ref, ya_ref, yr_ref, wo_ref, l1g_ref, l1b_ref, wug_ref, wuv_ref,
                           cw_ref, cb_ref, wd_ref, l2g_ref, l2b_ref,
                           qkvT_ref, lfns_ref, gcol_ref, *rest,
                           tm, tiles_per_seq, nj, n_pages):
    k_refs = rest[:n_pages]
    v_refs = rest[n_pages:2 * n_pages]
    lf_refs = rest[2 * n_pages:3 * n_pages]
    (o_ref, cf_ref, yas_ref, x1_scr, x1b_scr, acc_scr, gbuf, carry_scr,
     qb_scr, m_scr, l_scr, sacc_scr, c_scr) = rest[3 * n_pages:]
    del pt_ref
    i = pl.program_id(0)
    j = pl.program_id(1)

    @pl.when(j == 0)
    def _():
        _mix_head(x_ref, ya_ref, yr_ref, wo_ref, l1g_ref, l1b_ref, x1_scr, x1b_scr, acc_scr)
        _sample_attn_init(i, qkvT_ref, qb_scr, m_scr, l_scr, sacc_scr, c_scr)

    x1b = x1b_scr[...]
    gate = _dot(x1b, wug_ref[...])
    val = _dot(x1b, wuv_ref[...])

    first = (i % tiles_per_seq) == 0
    prev = jnp.where(first, 0.0, carry_scr[j])
    gbuf[pl.ds(0, SUBLANES), :] = prev
    gbuf[pl.ds(SUBLANES, tm), :] = gate
    gc = (cb_ref[...] + cw_ref[2:3, :] * gate
          + cw_ref[1:2, :] * gbuf[pl.ds(SUBLANES - 1, tm), :]
          + cw_ref[0:1, :] * gbuf[pl.ds(SUBLANES - 2, tm), :])
    tail = gbuf[pl.ds(tm, SUBLANES), :]
    carry_scr[j] = tail
    cf_ref[...] = tail[SUBLANES - 2:, :]

    hid = (_gelu(gc) * val).astype(BF16)
    acc_scr[...] += _dot(hid, wd_ref[j])

    _sample_attn_pages(k_refs, v_refs, lf_refs, qb_scr, m_scr, l_scr, sacc_scr, c_scr)

    @pl.when(j == nj - 1)
    def _():
        o_ref[...] = _layer_norm(DN_ALPHA * x1_scr[...] + acc_scr[...], l2g_ref[...], l2b_ref[...])
        _sample_attn_finish(i, qkvT_ref, lfns_ref, gcol_ref, yas_ref,
                            qb_scr, m_scr, l_scr, sacc_scr, c_scr)


def _prompt_mix_ffn(layer, x, ya, yr, wo, l1g, l1b, wu, cw, cb, wd, l2g, l2b, seq,
                    page_table, qkvT, lf_new, g_col, cache_kT, cache_vT, cache_lfT):
    M = x.shape[0]
    tm, fc = TM_FFN, FC
    nj = D_FF // fc
    tiles_per_seq = seq // tm
    n_s, n_pt = page_table.shape
    assert M // tm == n_s and n_pt % nj == 0, "one sample sequence per row tile, pages split over chunks"
    P = n_pt // nj
    kern = functools.partial(_prompt_mix_ffn_kernel, tm=tm, tiles_per_seq=tiles_per_seq, nj=nj,
                             n_pages=P)
    row = lambda i, j, pt: (i, 0)
    const = lambda i, j, pt: (0, 0)
    seq3 = lambda i, j, pt: (i, 0, 0)

    def page_spec(k, shape):
        nd = len(shape)
        return pl.BlockSpec((None, None) + shape,
                            lambda i, j, pt, k=k: (layer, pt[i, j * P + k]) + (0,) * nd)

    kv_shape = (N_HEADS, HEAD_DIM, PAGE_SIZE)
    in_specs = [
        pl.BlockSpec((tm, D_MODEL), row),
        pl.BlockSpec((tm, D_ATT), row),
        pl.BlockSpec((tm, D_RNN), row),
        pl.BlockSpec((D_MODEL, D_MODEL), const, pipeline_mode=pl.Buffered(1)),
        pl.BlockSpec((1, D_MODEL), const),
        pl.BlockSpec((1, D_MODEL), const),
        pl.BlockSpec((D_MODEL, fc), lambda i, j, pt: (0, j)),
        pl.BlockSpec((D_MODEL, fc), lambda i, j, pt: (0, j + nj)),
        pl.BlockSpec((3, fc), lambda i, j, pt: (0, j)),
        pl.BlockSpec((1, fc), lambda i, j, pt: (0, j)),
        pl.BlockSpec((nj, fc, D_MODEL), lambda i, j, pt: (0, 0, 0), pipeline_mode=pl.Buffered(1)),
        pl.BlockSpec((1, D_MODEL), const),
        pl.BlockSpec((1, D_MODEL), const),
        pl.BlockSpec((3 * D_ATT, n_s), const),
        pl.BlockSpec((None, N_HEADS, 1), seq3),
        pl.BlockSpec((D_ATT, 1), const),
    ]
    in_specs += [page_spec(k, kv_shape) for k in range(P)]
    in_specs += [page_spec(k, kv_shape) for k in range(P)]
    in_specs += [page_spec(k, (N_HEADS, PAGE_SIZE)) for k in range(P)]
    grid_spec = pltpu.PrefetchScalarGridSpec(
        num_scalar_prefetch=1,
        grid=(M // tm, nj),
        in_specs=in_specs,
        out_specs=[
            pl.BlockSpec((tm, D_MODEL), row),
            pl.BlockSpec((None, 2, fc), lambda i, j, pt: (i, 0, j)),
            pl.BlockSpec((None, 1, D_ATT), seq3),
        ],
        scratch_shapes=[
            pltpu.VMEM((tm, D_MODEL), F32),
            pltpu.VMEM((tm, D_MODEL), BF16),
            pltpu.VMEM((tm, D_MODEL), F32),
            pltpu.VMEM((tm + SUBLANES, fc), F32),
            pltpu.VMEM((nj, SUBLANES, fc), F32),
            pltpu.VMEM((N_HEADS, HEAD_DIM, PAGE_SIZE), F32),
            pltpu.VMEM((N_HEADS, 1, 1), F32),
            pltpu.VMEM((N_HEADS, 1, PAGE_SIZE), F32),
            pltpu.VMEM((N_HEADS, HEAD_DIM, PAGE_SIZE), F32),
            pltpu.VMEM((N_HEADS, 1), F32),
        ],
    )
    return pl.pallas_call(
        kern,
        grid_spec=grid_spec,
        out_shape=[
            jax.ShapeDtypeStruct((M, D_MODEL), F32),
            jax.ShapeDtypeStruct((M // tm, 2, D_FF), F32),
            jax.ShapeDtypeStruct((n_s, 1, D_ATT), BF16),
        ],
        compiler_params=pltpu.CompilerParams(
            dimension_semantics=("arbitrary", "arbitrary"), vmem_limit_bytes=VMEM_LIMIT),
        name="prompt_mix_ffn",
    )(page_table, x, ya, yr, wo, l1g, l1b, wu, wu, cw, cb, wd.reshape(nj, fc, D_MODEL), l2g, l2b,
      qkvT, lf_new, g_col,
      *([cache_kT] * P), *([cache_vT] * P), *([cache_lfT] * P))


def _sample_mix_ffn_kernel(x_ref, ya_ref, yr_ref, wo_ref, l1g_ref, l1b_ref, wug_ref, wuv_ref, p0_ref,
                           p1_ref, cw_ref, cb_ref, wd_ref, l2g_ref, l2b_ref,
                           o_ref, gate_ref,
                           x1_scr, x1b_scr, acc_scr, *, nj):
    j = pl.program_id(0)

    @pl.when(j == 0)
    def _():
        _mix_head(x_ref, ya_ref, yr_ref, wo_ref, l1g_ref, l1b_ref, x1_scr, x1b_scr, acc_scr)

    x1b = x1b_scr[...]
    gate = _dot(x1b, wug_ref[...])
    val = _dot(x1b, wuv_ref[...])
    gate_ref[...] = gate
    gc = (cb_ref[...] + cw_ref[2:3, :] * gate + cw_ref[1:2, :] * p1_ref[...]
          + cw_ref[0:1, :] * p0_ref[...])
    hid = (_gelu(gc) * val).astype(BF16)
    acc_scr[...] += _dot(hid, wd_ref[...])

    @pl.when(j == nj - 1)
    def _():
        o_ref[...] = _layer_norm(DN_ALPHA * x1_scr[...] + acc_scr[...], l2g_ref[...], l2b_ref[...])


def _sample_mix_ffn(x, ya, yr, wo, l1g, l1b, wu, p0, p1, cw, cb, wd, l2g, l2b):
    n = x.shape[0]
    fc = FC
    nj = D_FF // fc
    kern = functools.partial(_sample_mix_ffn_kernel, nj=nj)
    const = lambda j: (0, 0)
    chunk = lambda j: (0, j)
    return pl.pallas_call(
        kern,
        grid=(nj,),
        in_specs=[
            pl.BlockSpec((n, D_MODEL), const),
            pl.BlockSpec((n, D_ATT), const),
            pl.BlockSpec((n, D_RNN), const),
            pl.BlockSpec((D_MODEL, D_MODEL), const),
            pl.BlockSpec((1, D_MODEL), const),
            pl.BlockSpec((1, D_MODEL), const),
            pl.BlockSpec((D_MODEL, fc), chunk),
            pl.BlockSpec((D_MODEL, fc), lambda j: (0, j + nj)),
            pl.BlockSpec((n, fc), chunk),
            pl.BlockSpec((n, fc), chunk),
            pl.BlockSpec((3, fc), chunk),
            pl.BlockSpec((1, fc), chunk),
            pl.BlockSpec((fc, D_MODEL), lambda j: (j, 0)),
            pl.BlockSpec((1, D_MODEL), const),
            pl.BlockSpec((1, D_MODEL), const),
        ],
        out_specs=[
            pl.BlockSpec((n, D_MODEL), const),
            pl.BlockSpec((n, fc), chunk),
        ],
        out_shape=[
            jax.ShapeDtypeStruct((n, D_MODEL), F32),
            jax.ShapeDtypeStruct((n, D_FF), F32),
        ],
        scratch_shapes=[
            pltpu.VMEM((n, D_MODEL), F32),
            pltpu.VMEM((n, D_MODEL), BF16),
            pltpu.VMEM((n, D_MODEL), F32),
        ],
        compiler_params=pltpu.CompilerParams(
            dimension_semantics=("arbitrary",), vmem_limit_bytes=VMEM_LIMIT),
        name="sample_mix_ffn",
    )(x, ya, yr, wo, l1g, l1b, wu, wu, p0, p1, cw, cb, wd, l2g, l2b)


def _sample_inproj_kernel(x_ref, wn_ref, wt_ref, bf_ref, cs_ref, h0_ref, cw_ref, cb_ref, wg_ref, bg_ref,
                          lam_ref, grnn_ref,
                          qkvT_ref, k_ref, v_ref, lf_ref, yr_ref, h_ref, csn_ref):
    xb = x_ref[...].astype(BF16)
    yn = _dot(xb, wn_ref[...])
    yt = _dot_nt(xb, wt_ref[...])
    k_ref[...] = yt[:, D_ATT:2 * D_ATT]
    v_ref[...] = yt[:, 2 * D_ATT:3 * D_ATT]
    qkvT_ref[...] = _dot_nt(wt_ref[0:3 * D_ATT, :], xb)
    lf_ref[...] = _log_sigmoid(yt[:, 3 * D_ATT:3 * D_ATT + N_HEADS] + bf_ref[...])

    xr = yn[:, D_ATT + LANES:D_ATT + LANES + D_RNN]
    gr = yn[:, D_ATT + LANES + D_RNN:]
    xc = (cb_ref[...] + cw_ref[3:4, :] * xr + cw_ref[2:3, :] * cs_ref[2]
          + cw_ref[1:2, :] * cs_ref[1] + cw_ref[0:1, :] * cs_ref[0])
    csn_ref[0] = cs_ref[1]
    csn_ref[1] = cs_ref[2]
    csn_ref[2] = xr

    a, u = _rglru_gates(xc, wg_ref, bg_ref, lam_ref)
    h = a * h0_ref[...] + u
    h_ref[...] = h
    yr_ref[...] = _rms_norm(h * _gelu(gr), grnn_ref[...]).astype(BF16)


def _sample_inproj(x, wn, wt, bf_row, cs, h0, cw, cb, wg, bg, lam, grnn):
    n = x.shape[0]
    return pl.pallas_call(
        _sample_inproj_kernel,
        out_shape=[
            jax.ShapeDtypeStruct((3 * D_ATT, n), F32),
            jax.ShapeDtypeStruct((n, D_ATT), F32),
            jax.ShapeDtypeStruct((n, D_ATT), F32),
            jax.ShapeDtypeStruct((n, N_HEADS), F32),
            jax.ShapeDtypeStruct((n, D_RNN), BF16),
            jax.ShapeDtypeStruct((n, D_RNN), F32),
            jax.ShapeDtypeStruct((3, n, D_RNN), F32),
        ],
        compiler_params=pltpu.CompilerParams(vmem_limit_bytes=VMEM_LIMIT),
        name="sample_inproj",
    )(x, wn, wt, bf_row, cs, h0, cw, cb, wg, bg, lam, grnn)


def _column_on_lanes(mat, i):
    n = mat.shape[1]
    onehot = (lax.broadcasted_iota(jnp.int32, (n, LANES), 0) == i).astype(BF16)
    hi, mid, lo = _split3(mat)
    return (_dot(hi.astype(BF16), onehot) + _dot(mid.astype(BF16), onehot)
            + _dot(lo.astype(BF16), onehot))


def _sample_attn_init(i, qkvT_ref, qb_scr, m_scr, l_scr, acc_scr, c_scr):
    q = _column_on_lanes(qkvT_ref[0:D_ATT, :], i) * ATT_SCALE
    qb_scr[...] = q.reshape(N_HEADS, HEAD_DIM, PAGE_SIZE)
    m_scr[...] = jnp.full(m_scr.shape, NEG_BIG, F32)
    l_scr[...] = jnp.zeros(l_scr.shape, F32)
    acc_scr[...] = jnp.zeros(acc_scr.shape, F32)
    c_scr[...] = jnp.zeros(c_scr.shape, F32)


def _sample_attn_pages(k_refs, v_refs, lf_refs, qb_scr, m_scr, l_scr, acc_scr, c_scr):
    n_pages = len(k_refs)
    cs, run = _running_sum_blocks([lf_refs[i][...] for i in range(n_pages)], c_scr[...])
    c_scr[...] = run
    qb = qb_scr[...]
    scores = []
    for i in range(n_pages):
        s_i = jnp.sum(k_refs[i][...] * qb, axis=1, keepdims=True)
        scores.append(s_i - cs[i].reshape(N_HEADS, 1, PAGE_SIZE))
    s = jnp.concatenate(scores, axis=-1)

    m = m_scr[...]
    m_new = jnp.maximum(m, jnp.max(s, axis=-1, keepdims=True))
    alpha = jnp.exp(m - m_new)
    p = jnp.exp(s - m_new)
    l_new = alpha * l_scr[...]
    acc = alpha * acc_scr[...]
    for i in range(n_pages):
        p_i = p[:, :, i * PAGE_SIZE:(i + 1) * PAGE_SIZE]
        l_new = l_new + p_i
        acc = acc + v_refs[i][...] * p_i
    l_scr[...] = l_new
    acc_scr[...] = acc
    m_scr[...] = m_new


def _sample_attn_finish(i, qkvT_ref, lfn_ref, g_ref, o_ref, qb_scr, m_scr, l_scr, acc_scr, c_scr):
    hds = (N_HEADS, HEAD_DIM, LANES)
    c_new = (c_scr[...] + lfn_ref[...]).reshape(N_HEADS, 1, 1)
    kn = _column_on_lanes(qkvT_ref[D_ATT:2 * D_ATT, :], i).reshape(hds)
    vn = _column_on_lanes(qkvT_ref[2 * D_ATT:3 * D_ATT, :], i).reshape(hds)[:, :, 0:1]
    s_new = jnp.sum(qb_scr[...] * kn, axis=1, keepdims=True)[:, :, 0:1] - c_new
    m_old = m_scr[...]
    m_fin = jnp.maximum(m_old, s_new)
    a2 = jnp.exp(m_old - m_fin)
    p_new = jnp.exp(s_new - m_fin)
    l_fin = a2 * jnp.sum(l_scr[...], axis=-1, keepdims=True) + p_new
    o3 = (a2 * jnp.sum(acc_scr[...], axis=-1, keepdims=True) + p_new * vn) / l_fin
    y = o3.reshape(D_ATT, 1)
    ms = jnp.mean(y * y, axis=0, keepdims=True)
    yn = (y * lax.rsqrt(ms + EPS) * g_ref[...]).astype(BF16)
    e0 = (lax.broadcasted_iota(jnp.int32, (SUBLANES, LANES), 1) == 0).astype(BF16)
    row = _dot_nt(e0, jnp.broadcast_to(yn, (D_ATT, LANES)))
    o_ref[...] = row[0:1, :].astype(BF16)


def _block_diag(w):
    n, c, d = w.shape
    return jnp.einsum("ncd,nm->ncmd", w, jnp.eye(n, dtype=w.dtype)).reshape(n * c, n * d)


def kernel(x_prompt, x_sample, cache_k, cache_v, cache_logf, page_table, state_h, state_conv_rnn,
           state_conv_ffn, w_in, b_f, rnn_conv_w, rnn_conv_b, w_a, b_a, w_x, b_x, lam, g_att, g_rnn,
           w_out, ln1_g, ln1_b, w_up, ffn_conv_w, ffn_conv_b, w_down, ln2_g, ln2_b):
    B, T, _ = x_prompt.shape
    n_s = x_sample.shape[0]

    cache_kT = jnp.transpose(cache_k, (0, 1, 3, 4, 2))
    cache_vT = jnp.transpose(cache_v, (0, 1, 3, 4, 2))
    cache_lfT = jnp.transpose(cache_logf, (0, 1, 3, 2))

    x_p = x_prompt
    x_s = x_sample.reshape(n_s, D_MODEL)
    outs = [[] for _ in range(12)]
    stacked = None
    c1, c2, c3 = D_ATT, 2 * D_ATT, 3 * D_ATT
    c4 = c3 + N_HEADS
    c5 = c4 + D_RNN
    for l in range(DEPTH):
        wi = w_in[l]
        wn = jnp.concatenate([wi[:, c1:c2], wi[:, c3:c4], jnp.zeros((D_MODEL, LANES - N_HEADS), F32),
                              wi[:, c4:c5], wi[:, c5:]], axis=1).astype(BF16)
        wt = jnp.concatenate([wi[:, :c4], jnp.zeros((D_MODEL, 8), F32)], axis=1).T.astype(BF16)
        bf_row = jnp.concatenate([b_f[l], jnp.zeros((LANES - N_HEADS,), F32)])[None, :]
        wg = jnp.concatenate([_block_diag(w_a[l]), _block_diag(w_x[l])], axis=1).astype(BF16)
        bg = jnp.concatenate([b_a[l], b_x[l]])[None, :]
        wo = w_out[l].astype(BF16)
        wu = w_up[l].astype(BF16)
        wd = w_down[l].astype(BF16)
        row = lambda v: v[None, :]
        cw, cb = rnn_conv_w[l], row(rnn_conv_b[l])
        lam_l, grnn, g_col = row(lam[l]), row(g_rnn[l]), g_att[l][:, None]
        l1g, l1b, l2g, l2b = row(ln1_g[l]), row(ln1_b[l]), row(ln2_g[l]), row(ln2_b[l])
        fcw, fcb = ffn_conv_w[l], row(ffn_conv_b[l])

        cs = jnp.transpose(state_conv_rnn[l], (1, 0, 2))
        qkvT, ks, vs, lfs, yrs, hs, csn = _sample_inproj(
            x_s, wn, wt, b_f[l][None, :], cs, state_h[l], cw, cb, wg, bg, lam_l, grnn)

        qblk, kT_all, vT_all, kblk, vblk, lfT_all, yr, h_last, cr = _prompt_inproj(
            l, x_p, wn, wt, b_f[l][:, None], bf_row, cw, cb, wg, bg, lam_l, grnn, stacked)
        stacked = (kT_all, vT_all, lfT_all)
        ya = _prompt_attn(qblk, kblk, vblk, g_col)
        x2, cf, yas = _prompt_mix_ffn(
            l, x_p.reshape(B * T, D_MODEL), ya.reshape(B * T, D_ATT), yr.reshape(B * T, D_RNN),
            wo, l1g, l1b, wu, fcw, fcb, wd, l2g, l2b, T,
            page_table, qkvT, lfs.reshape(n_s, N_HEADS, 1), g_col,
            cache_kT, cache_vT, cache_lfT)
        x_p = x2.reshape(B, T, D_MODEL)
        outs[3].append(h_last.reshape(B, D_RNN))
        outs[4].append(cr)
        tiles_per_seq = T // TM_FFN
        outs[5].append(cf[tiles_per_seq - 1::tiles_per_seq])

        p0 = state_conv_ffn[l][:, 0, :]
        p1 = state_conv_ffn[l][:, 1, :]
        x_s, gate_s = _sample_mix_ffn(x_s, yas.reshape(n_s, D_ATT), yrs, wo, l1g, l1b, wu, p0, p1,
                                      fcw, fcb, wd, l2g, l2b)
        outs[6].append(ks.reshape(n_s, 1, N_HEADS, HEAD_DIM))
        outs[7].append(vs.reshape(n_s, 1, N_HEADS, HEAD_DIM))
        outs[8].append(lfs.reshape(n_s, 1, N_HEADS))
        outs[9].append(hs)
        outs[10].append(jnp.transpose(csn, (1, 0, 2)))
        outs[11].append(jnp.stack([p1, gate_s], axis=1))

    st = [jnp.stack(o) for o in outs[3:]]
    kT_all, vT_all, lfT_all = stacked
    k_p = kT_all.reshape(DEPTH, B, N_HEADS, HEAD_DIM, T).transpose(0, 1, 4, 2, 3)
    v_p = vT_all.reshape(DEPTH, B, N_HEADS, HEAD_DIM, T).transpose(0, 1, 4, 2, 3)
    lf_p = lfT_all.transpose(0, 1, 3, 2)
    return (x_p, x_s.reshape(n_s, 1, D_MODEL), k_p, v_p, lf_p, *st)
```

```python
import functools

import jax
import jax.numpy as jnp
from jax import lax
from jax.experimental import pallas as pl
from jax.experimental.pallas import tpu as pltpu

F32 = jnp.float32
BF16 = jnp.bfloat16

D_MODEL = 1024
D_ATT = 512
D_RNN = 512
N_HEADS = 8
HEAD_DIM = 64
N_RNN_BLOCKS = 8
D_FF = 3072
RGLRU_C = 8.0
DEPTH = 2
PAGE_SIZE = 128
DN_ALPHA = (2 * DEPTH) ** 0.25
EPS = 1e-5
ATT_SCALE = HEAD_DIM ** -0.5
NEG_BIG = -1e30

LANES = 128
SUBLANES = 8
VMEM_LIMIT = 56 * 1024 * 1024

TM_PROJ = 1024
TQ = 256
TK = 128
TM_FFN = 512
FC = 768


def _softplus(x):
    return jnp.maximum(x, 0.0) + jnp.log1p(jnp.exp(-jnp.abs(x)))


def _log_sigmoid(x):
    return -_softplus(-x)


def _gelu(x):
    c = (2.0 / jnp.pi) ** 0.5
    return 0.5 * x * (1.0 + jnp.tanh(c * (x + 0.044715 * (x * x * x))))


def _layer_norm(x, g, b):
    mu = jnp.mean(x, axis=-1, keepdims=True)
    xc = x - mu
    var = jnp.mean(xc * xc, axis=-1, keepdims=True)
    return xc * lax.rsqrt(var + EPS) * g + b


def _rms_norm(x, g):
    return x * lax.rsqrt(jnp.mean(x * x, axis=-1, keepdims=True) + EPS) * g


def _dot(a, b):
    return jnp.dot(a, b, preferred_element_type=F32)


def _dot_nt(a, b):
    return lax.dot_general(a, b, (((1,), (1,)), ((), ())), preferred_element_type=F32)


def _prefix_sum_lanes(x, tri):
    hi = x.astype(BF16)
    r1 = x - hi.astype(F32)
    mid = r1.astype(BF16)
    lo = (r1 - mid.astype(F32)).astype(BF16)
    return _dot(hi, tri) + _dot(mid, tri) + _dot(lo, tri)


def _tri128():
    r = lax.broadcasted_iota(jnp.int32, (LANES, LANES), 0)
    c = lax.broadcasted_iota(jnp.int32, (LANES, LANES), 1)
    return (r <= c).astype(BF16)


def _running_sum_blocks(blocks, run):
    tri = _tri128()
    pre = [_prefix_sum_lanes(b, tri) for b in blocks]
    out = []
    for p in pre:
        out.append(p + run)
        run = run + p[:, LANES - 1:LANES]
    return out, run


def _split3(x):
    hi = x.astype(BF16).astype(F32)
    r1 = x - hi
    mid = r1.astype(BF16).astype(F32)
    lo = (r1 - mid).astype(BF16).astype(F32)
    return hi, mid, lo


def _rglru_gates(xc, wg_ref, bg_ref, lam_ref):
    g = _dot(xc.astype(BF16), wg_ref[...]) + bg_ref[...]
    r = jax.nn.sigmoid(g[:, :D_RNN])
    i = jax.nn.sigmoid(g[:, D_RNN:])
    log_a = (-RGLRU_C * _softplus(-lam_ref[...])) * r
    a = jnp.exp(log_a)
    mult = jnp.sqrt(-jnp.tanh(log_a) * (a * a + 1.0))
    return a, mult * (i * xc)


def _prompt_inproj_kernel(*refs, tm, tq, tk, n_alias):
    (x_ref, wn_ref, wt_ref, bfc_ref, bfr_ref, cw_ref, cb_ref, wg_ref, bg_ref, lam_ref,
     grnn_ref) = refs[:11]
    (qblk_ref, kT_ref, vT_ref, kblk_ref, vblk_ref, lfT_ref, yr_ref, hl_ref, cr_ref,
     xbuf, a_scr, u_scr, h_scr, hc_scr, crun_scr) = refs[11 + n_alias:]
    t = pl.program_id(1)

    @pl.when(t == 0)
    def _():
        xbuf[pl.ds(0, SUBLANES), :] = jnp.zeros((SUBLANES, D_RNN), F32)
        hc_scr[...] = jnp.zeros((SUBLANES, D_RNN), F32)
        crun_scr[...] = jnp.zeros(crun_scr.shape, F32)

    xb = x_ref[...].astype(BF16)
    yn = _dot(xb, wn_ref[...])
    yt = _dot_nt(wt_ref[...], xb)

    qT = yt[0:D_ATT] * ATT_SCALE
    kT = yt[D_ATT:2 * D_ATT]
    vT = yt[2 * D_ATT:3 * D_ATT]
    kT_ref[...] = kT
    vT_ref[...] = vT
    lfT_ref[...] = _log_sigmoid(yt[3 * D_ATT:3 * D_ATT + N_HEADS] + bfc_ref[...])

    lf_rows = _log_sigmoid(yn[:, D_ATT:D_ATT + LANES] + bfr_ref[...])
    ri = lax.broadcasted_iota(jnp.int32, (LANES, LANES), 0)
    ci = lax.broadcasted_iota(jnp.int32, (LANES, LANES), 1)
    lower = (ci <= ri).astype(BF16)
    run = crun_scr[0:1, :]
    c_rows = []
    for j in range(tm // LANES):
        hi, mid, lo = _split3(lf_rows[j * LANES:(j + 1) * LANES])
        pre = (_dot(lower, hi.astype(BF16)) + _dot(lower, mid.astype(BF16))
               + _dot(lower, lo.astype(BF16)))
        c_rows.append(pre + run)
        run = run + pre[LANES - 1:LANES, :]
    crun_scr[...] = jnp.broadcast_to(run, crun_scr.shape)
    nhi, nmid, nlo = _split3(-jnp.concatenate(c_rows, axis=0))

    lane = lax.broadcasted_iota(jnp.int32, (tm, LANES), 1)
    is_head = lane < N_HEADS
    packed = jnp.where(is_head, nhi, 0.0)
    packed = packed + pltpu.roll(jnp.where(is_head, nmid, 0.0), N_HEADS, 1)
    packed = packed + pltpu.roll(jnp.where(is_head, nlo, 0.0), 2 * N_HEADS, 1)
    src = lax.broadcasted_iota(jnp.int32, (LANES, N_HEADS * LANES), 0)
    dst = lax.broadcasted_iota(jnp.int32, (LANES, N_HEADS * LANES), 1)
    src_head = src % N_HEADS
    src_term = src // N_HEADS
    base = jnp.where(src_head % 2 == 0, HEAD_DIM, 0)
    place = ((dst // LANES == src_head) & (dst % LANES == base + src_term)
             & (src_term < 3)).astype(BF16)
    bias_lanes = _dot(packed.astype(BF16), place)

    ones = jnp.ones((HEAD_DIM, tk), BF16)
    rowi = lax.broadcasted_iota(jnp.int32, (HEAD_DIM, tq), 0)
    sel_rows = jnp.where(rowi < 3, 1.0, 0.0).astype(BF16)
    for h in range(N_HEADS):
        pair = yn[:, (h // 2) * LANES:(h // 2 + 1) * LANES]
        own = (lane < HEAD_DIM) if h % 2 == 0 else (lane >= HEAD_DIM)
        kb = (jnp.where(own, pair, 0.0) + bias_lanes[:, h * LANES:(h + 1) * LANES]).astype(BF16)
        chan = slice(0, HEAD_DIM) if h % 2 == 0 else slice(HEAD_DIM, 2 * HEAD_DIM)
        rest = slice(HEAD_DIM, 2 * HEAD_DIM) if h % 2 == 0 else slice(0, HEAD_DIM)
        hrows = slice(h * HEAD_DIM, (h + 1) * HEAD_DIM)
        for c in range(tm // tk):
            cols = slice(c * tk, (c + 1) * tk)
            kblk_ref[c, h] = kb[cols, :]
            vblk_ref[c, h, chan, :] = vT[hrows, cols].astype(BF16)
            vblk_ref[c, h, rest, :] = ones
        for c in range(tm // tq):
            cols = slice(c * tq, (c + 1) * tq)
            qblk_ref[c, h, chan, :] = qT[hrows, cols].astype(BF16)
            qblk_ref[c, h, rest, :] = sel_rows

    xr = yn[:, D_ATT + LANES:D_ATT + LANES + D_RNN]
    gr = yn[:, D_ATT + LANES + D_RNN:]
    xbuf[pl.ds(SUBLANES, tm), :] = xr
    xc = (cb_ref[...] + cw_ref[3:4, :] * xr
          + cw_ref[2:3, :] * xbuf[pl.ds(SUBLANES - 1, tm), :]
          + cw_ref[1:2, :] * xbuf[pl.ds(SUBLANES - 2, tm), :]
          + cw_ref[0:1, :] * xbuf[pl.ds(SUBLANES - 3, tm), :])
    tail = xbuf[pl.ds(tm, SUBLANES), :]
    xbuf[pl.ds(0, SUBLANES), :] = tail
    cr_ref[...] = tail[SUBLANES - 3:, :]

    a, u = _rglru_gates(xc, wg_ref, bg_ref, lam_ref)
    a_scr[...] = a
    u_scr[...] = u

    sub = lax.broadcasted_iota(jnp.int32, (SUBLANES, D_RNN), 0)

    def group(g, hin):
        r0 = g * SUBLANES
        ag = a_scr[pl.ds(r0, SUBLANES), :]
        ug = u_scr[pl.ds(r0, SUBLANES), :]
        for d in (1, 2, 4):
            keep = sub >= d
            a_sh = pltpu.roll(ag, d, 0)
            u_sh = pltpu.roll(ug, d, 0)
            ug = jnp.where(keep, ag * u_sh + ug, ug)
            ag = jnp.where(keep, ag * a_sh, ag)
        hg = ag * hin + ug
        h_scr[pl.ds(r0, SUBLANES), :] = hg
        return jnp.broadcast_to(hg[SUBLANES - 1:SUBLANES, :], (SUBLANES, D_RNN))

    hin = hc_scr[...]
    for g in range(tm // SUBLANES):
        hin = group(g, hin)
    hc_scr[...] = hin
    hl_ref[...] = hin[0:1, :]

    y = h_scr[...] * _gelu(gr)
    yr_ref[...] = _rms_norm(y, grnn_ref[...]).astype(BF16)


def _prompt_inproj(layer, x, wn, wt, bf_col, bf_row, cw, cb, wg, bg, lam, grnn, stacked):
    B, T, _ = x.shape
    tm, tq, tk = TM_PROJ, TQ, TK
    nt = T // tm
    wt_rows = wt.shape[0]
    wn_cols = wn.shape[1]
    const = lambda b, t: (0, 0)
    n_alias = 0 if stacked is None else 3
    kern = functools.partial(_prompt_inproj_kernel, tm=tm, tq=tq, tk=tk, n_alias=n_alias)
    in_specs = [
        pl.BlockSpec((None, tm, D_MODEL), lambda b, t: (b, t, 0)),
        pl.BlockSpec((D_MODEL, wn_cols), const),
        pl.BlockSpec((wt_rows, D_MODEL), const),
        pl.BlockSpec((N_HEADS, 1), const),
        pl.BlockSpec((1, LANES), const),
        pl.BlockSpec((4, D_RNN), const),
        pl.BlockSpec((1, D_RNN), const),
        pl.BlockSpec((D_RNN, 2 * D_RNN), const),
        pl.BlockSpec((1, 2 * D_RNN), const),
        pl.BlockSpec((1, D_RNN), const),
        pl.BlockSpec((1, D_RNN), const),
    ] + [pl.BlockSpec(memory_space=pl.ANY)] * n_alias
    args = (x, wn, wt, bf_col, bf_row, cw, cb, wg, bg, lam, grnn)
    args += () if stacked is None else tuple(stacked)
    q_blk = (None, tm // tq, N_HEADS, 2 * HEAD_DIM, tq)
    v_blk = (None, tm // tk, N_HEADS, 2 * HEAD_DIM, tk)
    k_blk = (None, tm // tk, N_HEADS, tk, 2 * HEAD_DIM)
    blk_map = lambda b, t: (b, t, 0, 0, 0)
    return pl.pallas_call(
        kern,
        grid=(B, nt),
        in_specs=in_specs,
        out_specs=[
            pl.BlockSpec(q_blk, blk_map),
            pl.BlockSpec((None, None, D_ATT, tm), lambda b, t: (layer, b, 0, t)),
            pl.BlockSpec((None, None, D_ATT, tm), lambda b, t: (layer, b, 0, t)),
            pl.BlockSpec(k_blk, blk_map),
            pl.BlockSpec(v_blk, blk_map),
            pl.BlockSpec((None, None, N_HEADS, tm), lambda b, t: (layer, b, 0, t)),
            pl.BlockSpec((None, tm, D_RNN), lambda b, t: (b, t, 0)),
            pl.BlockSpec((None, 1, D_RNN), lambda b, t: (b, 0, 0)),
            pl.BlockSpec((None, 3, D_RNN), lambda b, t: (b, 0, 0)),
        ],
        out_shape=[
            jax.ShapeDtypeStruct((B, T // tq, N_HEADS, 2 * HEAD_DIM, tq), BF16),
            jax.ShapeDtypeStruct((DEPTH, B, D_ATT, T), F32),
            jax.ShapeDtypeStruct((DEPTH, B, D_ATT, T), F32),
            jax.ShapeDtypeStruct((B, T // tk, N_HEADS, tk, 2 * HEAD_DIM), BF16),
            jax.ShapeDtypeStruct((B, T // tk, N_HEADS, 2 * HEAD_DIM, tk), BF16),
            jax.ShapeDtypeStruct((DEPTH, B, N_HEADS, T), F32),
            jax.ShapeDtypeStruct((B, T, D_RNN), BF16),
            jax.ShapeDtypeStruct((B, 1, D_RNN), F32),
            jax.ShapeDtypeStruct((B, 3, D_RNN), F32),
        ],
        scratch_shapes=[
            pltpu.VMEM((tm + SUBLANES, D_RNN), F32),
            pltpu.VMEM((tm, D_RNN), F32),
            pltpu.VMEM((tm, D_RNN), F32),
            pltpu.VMEM((tm, D_RNN), F32),
            pltpu.VMEM((SUBLANES, D_RNN), F32),
            pltpu.VMEM((N_HEADS, LANES), F32),
        ],
        input_output_aliases={} if stacked is None else {11: 1, 12: 2, 13: 5},
        compiler_params=pltpu.CompilerParams(
            dimension_semantics=("arbitrary", "arbitrary"), vmem_limit_bytes=VMEM_LIMIT),
        name="prompt_inproj",
    )(*args)


def _prompt_attn_kernel(q_ref, k_ref, v_ref, g_ref, o_ref, m_scr, acc_scr, *, tq, tk):
    qi = pl.program_id(1)
    kpq = tq // tk
    key = lax.broadcasted_iota(jnp.int32, (tk, tq), 0)
    qry = lax.broadcasted_iota(jnp.int32, (tk, tq), 1)

    for h in range(N_HEADS):
        m_scr[h] = jnp.full((1, tq), NEG_BIG, F32)
        acc_scr[h] = jnp.zeros((LANES, tq), F32)

    def tile(kj, diag):
        for h in range(N_HEADS):
            s = _dot(k_ref[kj, h], q_ref[h])
            if diag is not None:
                s = jnp.where(key + diag * tk <= qry, s, NEG_BIG)
            m = m_scr[h]
            m_new = jnp.maximum(m, jnp.max(s, axis=0, keepdims=True))
            alpha = jnp.exp(m - m_new)
            p = jnp.exp(s - m_new).astype(BF16)
            acc_scr[h] = alpha * acc_scr[h] + _dot(v_ref[kj, h], p)
            m_scr[h] = m_new

    def full_tiles(j, carry, n):
        for d in range(n):
            tile(j * n + d, None)
        return carry

    groups = qi
    lax.fori_loop(0, groups // 4, functools.partial(full_tiles, n=4 * kpq), 0)
    lax.fori_loop(2 * (groups // 4), 2 * (groups // 4) + (groups // 2) % 2,
                  functools.partial(full_tiles, n=2 * kpq), 0)
    lax.fori_loop(groups - groups % 2, groups, functools.partial(full_tiles, n=kpq), 0)
    for d in range(kpq):
        tile(qi * kpq + d, d)

    halves = []
    for h in range(N_HEADS):
        acc = acc_scr[h]
        if h % 2 == 0:
            halves.append(acc[:HEAD_DIM] / acc[HEAD_DIM:])
        else:
            halves.append(acc[HEAD_DIM:] / acc[:HEAD_DIM])
    yT = jnp.concatenate(halves, axis=0)
    ms = jnp.mean(yT * yT, axis=0, keepdims=True)
    ynT = (yT * lax.rsqrt(ms + EPS) * g_ref[...]).astype(BF16)
    er = lax.broadcasted_iota(jnp.int32, (tq, tq), 0)
    ec = lax.broadcasted_iota(jnp.int32, (tq, tq), 1)
    o_ref[...] = _dot_nt((er == ec).astype(BF16), ynT).astype(BF16)


def _prompt_attn(qblk, kblk, vblk, g_col):
    B, nq, _, _, tq = qblk.shape
    nk, tk = kblk.shape[1], kblk.shape[3]
    T = nq * tq
    kern = functools.partial(_prompt_attn_kernel, tq=tq, tk=tk)
    return pl.pallas_call(
        kern,
        grid=(B, nq),
        in_specs=[
            pl.BlockSpec((None, None, N_HEADS, 2 * HEAD_DIM, tq), lambda b, i: (b, i, 0, 0, 0)),
            pl.BlockSpec((None, nk, N_HEADS, tk, 2 * HEAD_DIM), lambda b, i: (b, 0, 0, 0, 0)),
            pl.BlockSpec((None, nk, N_HEADS, 2 * HEAD_DIM, tk), lambda b, i: (b, 0, 0, 0, 0)),
            pl.BlockSpec((D_ATT, 1), lambda b, i: (0, 0)),
        ],
        out_specs=pl.BlockSpec((None, tq, D_ATT), lambda b, i: (b, i, 0)),
        out_shape=jax.ShapeDtypeStruct((B, T, D_ATT), BF16),
        scratch_shapes=[
            pltpu.VMEM((N_HEADS, 1, tq), F32),
            pltpu.VMEM((N_HEADS, LANES, tq), F32),
        ],
        compiler_params=pltpu.CompilerParams(
            dimension_semantics=("arbitrary", "arbitrary"), vmem_limit_bytes=VMEM_LIMIT),
        name="prompt_attn",
    )(qblk, kblk, vblk, g_col)


def _mix_head(x_ref, ya_ref, yr_ref, wo_ref, l1g_ref, l1b_ref, x1_scr, x1b_scr, acc_scr):
    mix = _dot(ya_ref[...], wo_ref[0:D_ATT, :]) + _dot(yr_ref[...], wo_ref[D_ATT:, :])
    x1 = _layer_norm(DN_ALPHA * x_ref[...] + mix, l1g_ref[...], l1b_ref[...])
    x1_scr[...] = x1
    x1b_scr[...] = x1.astype(BF16)
    acc_scr[...] = jnp.zeros(acc_scr.shape, F32)


def _prompt_mix_ffn_kernel(pt_ref, x_ref, ya_ref, yr_ref, wo_ref, l1g_ref, l1b_ref, wu_ref,
                           cw_ref, cb_ref, wd_ref, l2g_ref, l2b_ref,
                           qkvT_ref, lfns_ref, gcol_ref, *rest,
                           tm, tiles_per_seq, nj, n_pages):
    k_refs = rest[:n_pages]
    v_refs = rest[n_pages:2 * n_pages]
    lf_refs = rest[2 * n_pages:3 * n_pages]
    (o_ref, cf_ref, yas_ref, x1_scr, x1b_scr, acc_scr, gbuf, carry_scr,
     qb_scr, m_scr, l_scr, sacc_scr, c_scr) = rest[3 * n_pages:]
    del pt_ref
    i = pl.program_id(0)
    j = pl.program_id(1)

    @pl.when(j == 0)
    def _():
        _mix_head(x_ref, ya_ref, yr_ref, wo_ref, l1g_ref, l1b_ref, x1_scr, x1b_scr, acc_scr)
        _sample_attn_init(i, qkvT_ref, qb_scr, m_scr, l_scr, sacc_scr, c_scr)

    _sample_attn_pages(k_refs, v_refs, lf_refs, qb_scr, m_scr, l_scr, sacc_scr, c_scr)

    fc = wu_ref.shape[-1] // 2
    x1b = x1b_scr[...]
    gate = _dot(x1b, wu_ref[:, :fc])
    val = _dot(x1b, wu_ref[:, fc:])

    first = (i % tiles_per_seq) == 0
    prev = jnp.where(first, 0.0, carry_scr[j])
    gbuf[pl.ds(0, SUBLANES), :] = prev
    gbuf[pl.ds(SUBLANES, tm), :] = gate
    cw = cw_ref[j]
    gc = (cb_ref[j] + cw[2:3, :] * gate
          + cw[1:2, :] * gbuf[pl.ds(SUBLANES - 1, tm), :]
          + cw[0:1, :] * gbuf[pl.ds(SUBLANES - 2, tm), :])
    tail = gbuf[pl.ds(tm, SUBLANES), :]
    carry_scr[j] = tail
    cf_ref[j] = tail[SUBLANES - 2:, :]

    hid = (_gelu(gc) * val).astype(BF16)
    acc_scr[...] += _dot(hid, wd_ref[j])

    @pl.when(j == nj - 1)
    def _():
        o_ref[...] = _layer_norm(DN_ALPHA * x1_scr[...] + acc_scr[...], l2g_ref[...], l2b_ref[...])
        _sample_attn_finish(i, qkvT_ref, lfns_ref, gcol_ref, yas_ref,
                            qb_scr, m_scr, l_scr, sacc_scr, c_scr)


def _prompt_mix_ffn(layer, x, ya, yr, wo, l1g, l1b, wu_c, cw_c, cb_c, wd, l2g, l2b, seq,
                    page_table, qkvT, lf_new, g_col, cache_kT, cache_vT, cache_lfT):
    M = x.shape[0]
    tm, fc = TM_FFN, FC
    nj = D_FF // fc
    tiles_per_seq = seq // tm
    n_s, n_pt = page_table.shape
    assert M // tm == n_s and n_pt % nj == 0, "one sample sequence per row tile, pages split over chunks"
    P = n_pt // nj
    kern = functools.partial(_prompt_mix_ffn_kernel, tm=tm, tiles_per_seq=tiles_per_seq, nj=nj,
                             n_pages=P)
    row = lambda i, j, pt: (i, 0)
    const = lambda i, j, pt: (0, 0)
    seq3 = lambda i, j, pt: (i, 0, 0)

    def page_spec(k, shape):
        nd = len(shape)
        return pl.BlockSpec((None, None) + shape,
                            lambda i, j, pt, k=k: (layer, pt[i, j * P + k]) + (0,) * nd)

    kv_shape = (N_HEADS, HEAD_DIM, PAGE_SIZE)
    in_specs = [
        pl.BlockSpec((tm, D_MODEL), row),
        pl.BlockSpec((tm, D_ATT), row),
        pl.BlockSpec((tm, D_RNN), row),
        pl.BlockSpec((D_MODEL, D_MODEL), const, pipeline_mode=pl.Buffered(1)),
        pl.BlockSpec((1, D_MODEL), const),
        pl.BlockSpec((1, D_MODEL), const),
        pl.BlockSpec((None, D_MODEL, 2 * fc), lambda i, j, pt: (j, 0, 0)),
        pl.BlockSpec((nj, 3, fc), lambda i, j, pt: (0, 0, 0)),
        pl.BlockSpec((nj, 1, fc), lambda i, j, pt: (0, 0, 0)),
        pl.BlockSpec((nj, fc, D_MODEL), lambda i, j, pt: (0, 0, 0), pipeline_mode=pl.Buffered(1)),
        pl.BlockSpec((1, D_MODEL), const),
        pl.BlockSpec((1, D_MODEL), const),
        pl.BlockSpec((3 * D_ATT, n_s), const),
        pl.BlockSpec((None, N_HEADS, 1), seq3),
        pl.BlockSpec((D_ATT, 1), const),
    ]
    in_specs += [page_spec(k, kv_shape) for k in range(P)]
    in_specs += [page_spec(k, kv_shape) for k in range(P)]
    in_specs += [page_spec(k, (N_HEADS, PAGE_SIZE)) for k in range(P)]
    grid_spec = pltpu.PrefetchScalarGridSpec(
        num_scalar_prefetch=1,
        grid=(M // tm, nj),
        in_specs=in_specs,
        out_specs=[
            pl.BlockSpec((tm, D_MODEL), row),
            pl.BlockSpec((None, nj, 2, fc), lambda i, j, pt: (i, 0, 0, 0)),
            pl.BlockSpec((None, 1, D_ATT), seq3),
        ],
        scratch_shapes=[
            pltpu.VMEM((tm, D_MODEL), F32),
            pltpu.VMEM((tm, D_MODEL), BF16),
            pltpu.VMEM((tm, D_MODEL), F32),
            pltpu.VMEM((tm + SUBLANES, fc), F32),
            pltpu.VMEM((nj, SUBLANES, fc), F32),
            pltpu.VMEM((N_HEADS, HEAD_DIM, PAGE_SIZE), F32),
            pltpu.VMEM((N_HEADS, 1, 1), F32),
            pltpu.VMEM((N_HEADS, 1, PAGE_SIZE), F32),
            pltpu.VMEM((N_HEADS, HEAD_DIM, PAGE_SIZE), F32),
            pltpu.VMEM((N_HEADS, 1), F32),
        ],
    )
    return pl.pallas_call(
        kern,
        grid_spec=grid_spec,
        out_shape=[
            jax.ShapeDtypeStruct((M, D_MODEL), F32),
            jax.ShapeDtypeStruct((M // tm, nj, 2, fc), F32),
            jax.ShapeDtypeStruct((n_s, 1, D_ATT), BF16),
        ],
        compiler_params=pltpu.CompilerParams(
            dimension_semantics=("arbitrary", "arbitrary"), vmem_limit_bytes=VMEM_LIMIT),
        name="prompt_mix_ffn",
    )(page_table, x, ya, yr, wo, l1g, l1b, wu_c, cw_c, cb_c, wd.reshape(nj, fc, D_MODEL), l2g, l2b,
      qkvT, lf_new, g_col,
      *([cache_kT] * P), *([cache_vT] * P), *([cache_lfT] * P))


def _sample_mix_ffn_kernel(x_ref, ya_ref, yr_ref, wo_ref, l1g_ref, l1b_ref, wu_ref, p0_ref,
                           p1_ref, cw_ref, cb_ref, wd_ref, l2g_ref, l2b_ref,
                           o_ref, gate_ref,
                           x1_scr, x1b_scr, acc_scr, *, nj):
    j = pl.program_id(0)

    @pl.when(j == 0)
    def _():
        _mix_head(x_ref, ya_ref, yr_ref, wo_ref, l1g_ref, l1b_ref, x1_scr, x1b_scr, acc_scr)

    fc = wu_ref.shape[-1] // 2
    x1b = x1b_scr[...]
    gate = _dot(x1b, wu_ref[:, :fc])
    val = _dot(x1b, wu_ref[:, fc:])
    gate_ref[...] = gate
    cw = cw_ref[j]
    gc = cb_ref[j] + cw[2:3, :] * gate + cw[1:2, :] * p1_ref[...] + cw[0:1, :] * p0_ref[...]
    hid = (_gelu(gc) * val).astype(BF16)
    acc_scr[...] += _dot(hid, wd_ref[...])

    @pl.when(j == nj - 1)
    def _():
        o_ref[...] = _layer_norm(DN_ALPHA * x1_scr[...] + acc_scr[...], l2g_ref[...], l2b_ref[...])


def _sample_mix_ffn(x, ya, yr, wo, l1g, l1b, wu_c, p0, p1, cw_c, cb_c, wd, l2g, l2b):
    n = x.shape[0]
    fc = FC
    nj = D_FF // fc
    kern = functools.partial(_sample_mix_ffn_kernel, nj=nj)
    const = lambda j: (0, 0)
    chunk = lambda j: (0, j)
    return pl.pallas_call(
        kern,
        grid=(nj,),
        in_specs=[
            pl.BlockSpec((n, D_MODEL), const),
            pl.BlockSpec((n, D_ATT), const),
            pl.BlockSpec((n, D_RNN), const),
            pl.BlockSpec((D_MODEL, D_MODEL), const),
            pl.BlockSpec((1, D_MODEL), const),
            pl.BlockSpec((1, D_MODEL), const),
            pl.BlockSpec((None, D_MODEL, 2 * fc), lambda j: (j, 0, 0)),
            pl.BlockSpec((n, fc), chunk),
            pl.BlockSpec((n, fc), chunk),
            pl.BlockSpec((nj, 3, fc), lambda j: (0, 0, 0)),
            pl.BlockSpec((nj, 1, fc), lambda j: (0, 0, 0)),
            pl.BlockSpec((fc, D_MODEL), lambda j: (j, 0)),
            pl.BlockSpec((1, D_MODEL), const),
            pl.BlockSpec((1, D_MODEL), const),
        ],
        out_specs=[
            pl.BlockSpec((n, D_MODEL), const),
            pl.BlockSpec((n, fc), chunk),
        ],
        out_shape=[
            jax.ShapeDtypeStruct((n, D_MODEL), F32),
            jax.ShapeDtypeStruct((n, D_FF), F32),
        ],
        scratch_shapes=[
            pltpu.VMEM((n, D_MODEL), F32),
            pltpu.VMEM((n, D_MODEL), BF16),
            pltpu.VMEM((n, D_MODEL), F32),
        ],
        compiler_params=pltpu.CompilerParams(
            dimension_semantics=("arbitrary",), vmem_limit_bytes=VMEM_LIMIT),
        name="sample_mix_ffn",
    )(x, ya, yr, wo, l1g, l1b, wu_c, p0, p1, cw_c, cb_c, wd, l2g, l2b)


def _sample_inproj_kernel(x_ref, wn_ref, wt_ref, bf_ref, cs_ref, h0_ref, cw_ref, cb_ref, wg_ref, bg_ref,
                          lam_ref, grnn_ref,
                          qkvT_ref, k_ref, v_ref, lf_ref, yr_ref, h_ref, csn_ref):
    xb = x_ref[...].astype(BF16)
    yn = _dot(xb, wn_ref[...])
    yt = _dot_nt(xb, wt_ref[...])
    k_ref[...] = yt[:, D_ATT:2 * D_ATT]
    v_ref[...] = yt[:, 2 * D_ATT:3 * D_ATT]
    qkvT_ref[...] = _dot_nt(wt_ref[0:3 * D_ATT, :], xb)
    lf_ref[...] = _log_sigmoid(yt[:, 3 * D_ATT:3 * D_ATT + N_HEADS] + bf_ref[...])

    xr = yn[:, D_ATT + LANES:D_ATT + LANES + D_RNN]
    gr = yn[:, D_ATT + LANES + D_RNN:]
    xc = (cb_ref[...] + cw_ref[3:4, :] * xr + cw_ref[2:3, :] * cs_ref[2]
          + cw_ref[1:2, :] * cs_ref[1] + cw_ref[0:1, :] * cs_ref[0])
    csn_ref[0] = cs_ref[1]
    csn_ref[1] = cs_ref[2]
    csn_ref[2] = xr

    a, u = _rglru_gates(xc, wg_ref, bg_ref, lam_ref)
    h = a * h0_ref[...] + u
    h_ref[...] = h
    yr_ref[...] = _rms_norm(h * _gelu(gr), grnn_ref[...]).astype(BF16)


def _sample_inproj(x, wn, wt, bf_row, cs, h0, cw, cb, wg, bg, lam, grnn):
    n = x.shape[0]
    return pl.pallas_call(
        _sample_inproj_kernel,
        out_shape=[
            jax.ShapeDtypeStruct((3 * D_ATT, n), F32),
            jax.ShapeDtypeStruct((n, D_ATT), F32),
            jax.ShapeDtypeStruct((n, D_ATT), F32),
            jax.ShapeDtypeStruct((n, N_HEADS), F32),
            jax.ShapeDtypeStruct((n, D_RNN), BF16),
            jax.ShapeDtypeStruct((n, D_RNN), F32),
            jax.ShapeDtypeStruct((3, n, D_RNN), F32),
        ],
        compiler_params=pltpu.CompilerParams(vmem_limit_bytes=VMEM_LIMIT),
        name="sample_inproj",
    )(x, wn, wt, bf_row, cs, h0, cw, cb, wg, bg, lam, grnn)


def _column_on_lanes(mat, i):
    n = mat.shape[1]
    onehot = (lax.broadcasted_iota(jnp.int32, (n, LANES), 0) == i).astype(BF16)
    hi, mid, lo = _split3(mat)
    return (_dot(hi.astype(BF16), onehot) + _dot(mid.astype(BF16), onehot)
            + _dot(lo.astype(BF16), onehot))


def _sample_attn_init(i, qkvT_ref, qb_scr, m_scr, l_scr, acc_scr, c_scr):
    q = _column_on_lanes(qkvT_ref[0:D_ATT, :], i) * ATT_SCALE
    qb_scr[...] = q.reshape(N_HEADS, HEAD_DIM, PAGE_SIZE)
    m_scr[...] = jnp.full(m_scr.shape, NEG_BIG, F32)
    l_scr[...] = jnp.zeros(l_scr.shape, F32)
    acc_scr[...] = jnp.zeros(acc_scr.shape, F32)
    c_scr[...] = jnp.zeros(c_scr.shape, F32)


def _sample_attn_pages(k_refs, v_refs, lf_refs, qb_scr, m_scr, l_scr, acc_scr, c_scr):
    n_pages = len(k_refs)
    cs, run = _running_sum_blocks([lf_refs[i][...] for i in range(n_pages)], c_scr[...])
    c_scr[...] = run
    qb = qb_scr[...]
    scores = []
    for i in range(n_pages):
        s_i = jnp.sum(k_refs[i][...] * qb, axis=1, keepdims=True)
        scores.append(s_i - cs[i].reshape(N_HEADS, 1, PAGE_SIZE))
    s = jnp.concatenate(scores, axis=-1)

    m = m_scr[...]
    m_new = jnp.maximum(m, jnp.max(s, axis=-1, keepdims=True))
    alpha = jnp.exp(m - m_new)
    p = jnp.exp(s - m_new)
    l_new = alpha * l_scr[...]
    acc = alpha * acc_scr[...]
    for i in range(n_pages):
        p_i = p[:, :, i * PAGE_SIZE:(i + 1) * PAGE_SIZE]
        l_new = l_new + p_i
        acc = acc + v_refs[i][...] * p_i
    l_scr[...] = l_new
    acc_scr[...] = acc
    m_scr[...] = m_new


def _sample_attn_finish(i, qkvT_ref, lfn_ref, g_ref, o_ref, qb_scr, m_scr, l_scr, acc_scr, c_scr):
    hds = (N_HEADS, HEAD_DIM, LANES)
    c_new = (c_scr[...] + lfn_ref[...]).reshape(N_HEADS, 1, 1)
    kn = _column_on_lanes(qkvT_ref[D_ATT:2 * D_ATT, :], i).reshape(hds)
    vn = _column_on_lanes(qkvT_ref[2 * D_ATT:3 * D_ATT, :], i).reshape(hds)[:, :, 0:1]
    s_new = jnp.sum(qb_scr[...] * kn, axis=1, keepdims=True)[:, :, 0:1] - c_new
    m_old = m_scr[...]
    m_fin = jnp.maximum(m_old, s_new)
    a2 = jnp.exp(m_old - m_fin)
    p_new = jnp.exp(s_new - m_fin)
    l_fin = a2 * jnp.sum(l_scr[...], axis=-1, keepdims=True) + p_new
    o3 = (a2 * jnp.sum(acc_scr[...], axis=-1, keepdims=True) + p_new * vn) / l_fin
    y = o3.reshape(D_ATT, 1)
    ms = jnp.mean(y * y, axis=0, keepdims=True)
    yn = (y * lax.rsqrt(ms + EPS) * g_ref[...]).astype(BF16)
    e0 = (lax.broadcasted_iota(jnp.int32, (SUBLANES, LANES), 1) == 0).astype(BF16)
    row = _dot_nt(e0, jnp.broadcast_to(yn, (D_ATT, LANES)))
    o_ref[...] = row[0:1, :].astype(BF16)


def _block_diag(w):
    n, c, d = w.shape
    return jnp.einsum("ncd,nm->ncmd", w, jnp.eye(n, dtype=w.dtype)).reshape(n * c, n * d)


def kernel(x_prompt, x_sample, cache_k, cache_v, cache_logf, page_table, state_h, state_conv_rnn,
           state_conv_ffn, w_in, b_f, rnn_conv_w, rnn_conv_b, w_a, b_a, w_x, b_x, lam, g_att, g_rnn,
           w_out, ln1_g, ln1_b, w_up, ffn_conv_w, ffn_conv_b, w_down, ln2_g, ln2_b):
    B, T, _ = x_prompt.shape
    n_s = x_sample.shape[0]

    cache_kT = jnp.transpose(cache_k, (0, 1, 3, 4, 2))
    cache_vT = jnp.transpose(cache_v, (0, 1, 3, 4, 2))
    cache_lfT = jnp.transpose(cache_logf, (0, 1, 3, 2))

    x_p = x_prompt
    x_s = x_sample.reshape(n_s, D_MODEL)
    outs = [[] for _ in range(12)]
    stacked = None
    c1, c2, c3 = D_ATT, 2 * D_ATT, 3 * D_ATT
    c4 = c3 + N_HEADS
    c5 = c4 + D_RNN
    for l in range(DEPTH):
        wi = w_in[l]
        wn = jnp.concatenate([wi[:, c1:c2], wi[:, c3:c4], jnp.zeros((D_MODEL, LANES - N_HEADS), F32),
                              wi[:, c4:c5], wi[:, c5:]], axis=1).astype(BF16)
        wt = jnp.concatenate([wi[:, :c4], jnp.zeros((D_MODEL, 8), F32)], axis=1).T.astype(BF16)
        bf_row = jnp.concatenate([b_f[l], jnp.zeros((LANES - N_HEADS,), F32)])[None, :]
        wg = jnp.concatenate([_block_diag(w_a[l]), _block_diag(w_x[l])], axis=1).astype(BF16)
        bg = jnp.concatenate([b_a[l], b_x[l]])[None, :]
        wo = w_out[l].astype(BF16)
        nj = D_FF // FC
        wu_c = jnp.concatenate([w_up[l][:, :D_FF].reshape(D_MODEL, nj, FC),
                                w_up[l][:, D_FF:].reshape(D_MODEL, nj, FC)], axis=2)
        wu_c = wu_c.transpose(1, 0, 2).astype(BF16)
        wd = w_down[l].astype(BF16)
        row = lambda v: v[None, :]
        cw, cb = rnn_conv_w[l], row(rnn_conv_b[l])
        lam_l, grnn, g_col = row(lam[l]), row(g_rnn[l]), g_att[l][:, None]
        l1g, l1b, l2g, l2b = row(ln1_g[l]), row(ln1_b[l]), row(ln2_g[l]), row(ln2_b[l])
        fcw = ffn_conv_w[l].reshape(3, nj, FC).transpose(1, 0, 2)
        fcb = ffn_conv_b[l].reshape(nj, 1, FC)

        cs = jnp.transpose(state_conv_rnn[l], (1, 0, 2))
        qkvT, ks, vs, lfs, yrs, hs, csn = _sample_inproj(
            x_s, wn, wt, b_f[l][None, :], cs, state_h[l], cw, cb, wg, bg, lam_l, grnn)

        qblk, kT_all, vT_all, kblk, vblk, lfT_all, yr, h_last, cr = _prompt_inproj(
            l, x_p, wn, wt, b_f[l][:, None], bf_row, cw, cb, wg, bg, lam_l, grnn, stacked)
        stacked = (kT_all, vT_all, lfT_all)
        ya = _prompt_attn(qblk, kblk, vblk, g_col)
        x2, cf, yas = _prompt_mix_ffn(
            l, x_p.reshape(B * T, D_MODEL), ya.reshape(B * T, D_ATT), yr.reshape(B * T, D_RNN),
            wo, l1g, l1b, wu_c, fcw, fcb, wd, l2g, l2b, T,
            page_table, qkvT, lfs.reshape(n_s, N_HEADS, 1), g_col,
            cache_kT, cache_vT, cache_lfT)
        x_p = x2.reshape(B, T, D_MODEL)
        outs[3].append(h_last.reshape(B, D_RNN))
        outs[4].append(cr)
        tiles_per_seq = T // TM_FFN
        outs[5].append(cf[tiles_per_seq - 1::tiles_per_seq].transpose(0, 2, 1, 3).reshape(B, 2, D_FF))

        p0 = state_conv_ffn[l][:, 0, :]
        p1 = state_conv_ffn[l][:, 1, :]
        x_s, gate_s = _sample_mix_ffn(x_s, yas.reshape(n_s, D_ATT), yrs, wo, l1g, l1b, wu_c, p0, p1,
                                      fcw, fcb, wd, l2g, l2b)
        outs[6].append(ks.reshape(n_s, 1, N_HEADS, HEAD_DIM))
        outs[7].append(vs.reshape(n_s, 1, N_HEADS, HEAD_DIM))
        outs[8].append(lfs.reshape(n_s, 1, N_HEADS))
        outs[9].append(hs)
        outs[10].append(jnp.transpose(csn, (1, 0, 2)))
        outs[11].append(jnp.stack([p1, gate_s], axis=1))

    st = [jnp.stack(o) for o in outs[3:]]
    kT_all, vT_all, lfT_all = stacked
    k_p = kT_all.reshape(DEPTH, B, N_HEADS, HEAD_DIM, T).transpose(0, 1, 4, 2, 3)
    v_p = vT_all.reshape(DEPTH, B, N_HEADS, HEAD_DIM, T).transpose(0, 1, 4, 2, 3)
    lf_p = lfT_all.transpose(0, 1, 3, 2)
    return (x_p, x_s.reshape(n_s, 1, D_MODEL), k_p, v_p, lf_p, *st)
```
